```python
import math
import jax, jax.numpy as jnp
from jax import lax
import numpy as np

D_MODEL = 2048
BATCH = 4
SEQ = 4096
DEPTH = 1

MIX_WIDTH = D_MODEL
MLSTM_HEADS = 4
MLSTM_WIDTH = MIX_WIDTH // 2
MLSTM_HEAD_DIM = MLSTM_WIDTH // MLSTM_HEADS
MLSTM_CHUNK = 64
CONV_WIDTH = 4
ATTN_HEADS = 8
ATTN_WIDTH = MIX_WIDTH - MLSTM_WIDTH
ATTN_HEAD_DIM = ATTN_WIDTH // ATTN_HEADS
DILATED_PATTERNS = ((128, 1), (512, 4), (2048, 16))
ROPE_THETA = 500000.0
ROT_DIM = ATTN_HEAD_DIM // 4
PEER_KEYS = 128
PEER_EXPERTS = PEER_KEYS * PEER_KEYS
PEER_HEADS = 8
PEER_TOPK = 16
PEER_QUERY_DIM = 256
PEER_TOKEN_BLOCK = 128
DEEPNORM_ALPHA = (2.0 * DEPTH) ** 0.25
DEEPNORM_BETA = (8.0 * DEPTH) ** -0.25
LN_EPS = 1e-5

PROJ_SIZES = (MLSTM_WIDTH, MLSTM_WIDTH, MLSTM_WIDTH, MLSTM_WIDTH, MLSTM_HEADS, MLSTM_HEADS,
              ATTN_WIDTH, ATTN_WIDTH, ATTN_WIDTH)
IN_COLS = sum(PROJ_SIZES)

kernel_name = "hymba_mlstm_dilated_peer_layer"


def layer_norm(x, g, b):
    xf = x.astype(jnp.float32)
    mu = jnp.mean(xf, axis=-1, keepdims=True)
    var = jnp.mean(jnp.square(xf - mu), axis=-1, keepdims=True)
    y = (xf - mu) * lax.rsqrt(var + LN_EPS)
    return (y * g.astype(jnp.float32) + b.astype(jnp.float32)).astype(x.dtype)


def causal_depthwise_conv(x, w, b):
    c = x.shape[-1]
    kern = w[:, None, :].astype(x.dtype)
    y = lax.conv_general_dilated(x, kern, window_strides=(1,), padding=((CONV_WIDTH - 1, 0),),
                                 dimension_numbers=('NWC', 'WIO', 'NWC'), feature_group_count=c)
    return y + b.astype(x.dtype)


def mlstm_chunkwise(q, k, v, i_pre, log_f):
    bsz, nh, s, d = q.shape
    nc = s // MLSTM_CHUNK
    def chunks(t):
        t = t.reshape(bsz, nh, nc, MLSTM_CHUNK, *t.shape[3:])
        return jnp.moveaxis(t, 2, 0)
    causal = jnp.tril(jnp.ones((MLSTM_CHUNK, MLSTM_CHUNK), dtype=bool))

    def step(carry, inp):
        c_st, n_st, m_st = carry
        qc, kc, vc, ig, lf = inp
        bcum = jnp.cumsum(lf, axis=-1)
        dmat = bcum[..., :, None] - bcum[..., None, :] + ig[..., None, :]
        dmat = jnp.where(causal, dmat, -jnp.inf)
        m_inter = bcum + m_st[..., None]
        m_t = jnp.maximum(m_inter, jnp.max(dmat, axis=-1))
        wts = jnp.einsum('bhtd,bhsd->bhts', qc, kc) * jnp.exp(dmat - m_t[..., None])
        inter = jnp.exp(m_inter - m_t)
        num = jnp.einsum('bhts,bhsv->bhtv', wts, vc) + inter[..., None] * jnp.einsum('bhvd,bhtd->bhtv', c_st, qc)
        den = jnp.sum(wts, axis=-1) + inter * jnp.einsum('bhd,bhtd->bht', n_st, qc)
        h = num / jnp.maximum(jnp.abs(den), jnp.exp(-m_t))[..., None]
        b_last = bcum[..., -1]
        g = b_last[..., None] - bcum + ig
        m_new = jnp.maximum(b_last + m_st, jnp.max(g, axis=-1))
        decay = jnp.exp(b_last + m_st - m_new)
        wk = jnp.exp(g - m_new[..., None])
        c_new = decay[..., None, None] * c_st + jnp.einsum('bhs,bhsv,bhsd->bhvd', wk, vc, kc)
        n_new = decay[..., None] * n_st + jnp.einsum('bhs,bhsd->bhd', wk, kc)
        return (c_new, n_new, m_new), h

    init = (jnp.zeros((bsz, nh, d, d), jnp.float32), jnp.zeros((bsz, nh, d), jnp.float32),
            jnp.zeros((bsz, nh), jnp.float32))
    _, h = lax.scan(step, init, (chunks(q), chunks(k), chunks(v), chunks(i_pre), chunks(log_f)))
    return jnp.moveaxis(h, 0, 2).reshape(bsz, nh, s, d)


def rope_partial(x, pos):
    half = ROT_DIM // 2
    inv = ROPE_THETA ** (-jnp.arange(half, dtype=jnp.float32) / half)
    ang = pos[:, None] * inv[None, :]
    cos = jnp.cos(ang).astype(x.dtype)
    sin = jnp.sin(ang).astype(x.dtype)
    x1, x2, rest = x[..., :half], x[..., half:ROT_DIM], x[..., ROT_DIM:]
    return jnp.concatenate([x1 * cos - x2 * sin, x1 * sin + x2 * cos, rest], axis=-1)


def banded_causal_attention(q, k, v, window):
    bsz, g, l, dh = q.shape
    blk = window
    nb = -(-l // blk)
    pad = nb * blk - l
    def blocks(t):
        return jnp.pad(t, ((0, 0), (0, 0), (0, pad), (0, 0))).reshape(bsz, g, nb, blk, dh)
    qb, kb, vb = blocks(q), blocks(k), blocks(v)
    def with_prev(t):
        prev = jnp.pad(t[:, :, :-1], ((0, 0), (0, 0), (1, 0), (0, 0), (0, 0)))
        return jnp.concatenate([prev, t], axis=3)
    kk, vv = with_prev(kb), with_prev(vb)
    s = jnp.einsum('bgnqd,bgnkd->bgnqk', qb, kk).astype(jnp.float32) * (dh ** -0.5)
    bi = jnp.arange(nb)[:, None, None]
    qpos = bi * blk + jnp.arange(blk)[None, :, None]
    kpos = (bi - 1) * blk + jnp.arange(2 * blk)[None, None, :]
    dist = qpos - kpos
    valid = (dist >= 0) & (dist <= window) & (kpos >= 0)
    s = jnp.where(valid, s, -jnp.inf)
    m = jnp.max(s, axis=-1, keepdims=True)
    p = jnp.exp(s - m)
    den = jnp.sum(p, axis=-1)
    o = jnp.einsum('bgnqk,bgnkd->bgnqd', p, vv.astype(jnp.float32)) / den[..., None]
    lse = m[..., 0] + jnp.log(den)
    o = o.reshape(bsz, g, nb * blk, dh)[:, :, :l]
    lse = lse.reshape(bsz, g, nb * blk)[:, :, :l]
    return o, lse


def dilated_attention(q, k, v):
    bsz, nh, s, dh = q.shape
    outs, lses = [], []
    for window, dil in DILATED_PATTERNS:
        l = s // dil
        def to_sub(t):
            return t.reshape(bsz, nh, l, dil, dh).transpose(0, 1, 3, 2, 4).reshape(bsz, nh * dil, l, dh)
        o, lse = banded_causal_attention(to_sub(q), to_sub(k), to_sub(v), window // dil)
        outs.append(o.reshape(bsz, nh, dil, l, dh).transpose(0, 1, 3, 2, 4).reshape(bsz, nh, s, dh))
        lses.append(lse.reshape(bsz, nh, dil, l).transpose(0, 1, 3, 2).reshape(bsz, nh, s))
    wts = jax.nn.softmax(jnp.stack(lses, axis=0), axis=0)
    out = jnp.sum(wts[..., None] * jnp.stack(outs, axis=0), axis=0)
    return out.astype(q.dtype)


def token_mixer(x, w_in, conv_w, conv_b, b_igate, b_fgate, mh_norm_g, w_out):
    bsz, s, _ = x.shape
    proj = x @ w_in
    idx = np.cumsum(PROJ_SIZES)[:-1].tolist()
    mq, mk, mv, mo, mi, mf, aq, ak, av = jnp.split(proj, idx, axis=-1)
    qk = jax.nn.silu(causal_depthwise_conv(jnp.concatenate([mq, mk], axis=-1), conv_w, conv_b))
    mq, mk = qk[..., :MLSTM_WIDTH], qk[..., MLSTM_WIDTH:]
    def mheads(t):
        return t.reshape(bsz, s, MLSTM_HEADS, MLSTM_HEAD_DIM).transpose(0, 2, 1, 3).astype(jnp.float32)
    qh = mheads(mq)
    kh = mheads(mk) * (MLSTM_HEAD_DIM ** -0.5)
    vh = mheads(mv)
    i_pre = (mi + b_igate).astype(jnp.float32).transpose(0, 2, 1)
    log_f = jax.nn.log_sigmoid((mf + b_fgate).astype(jnp.float32)).transpose(0, 2, 1)
    h = mlstm_chunkwise(qh, kh, vh, i_pre, log_f)
    mu = jnp.mean(h, axis=-1, keepdims=True)
    var = jnp.mean(jnp.square(h - mu), axis=-1, keepdims=True)
    h = (h - mu) * lax.rsqrt(var + LN_EPS) * mh_norm_g.reshape(MLSTM_HEADS, 1, MLSTM_HEAD_DIM).astype(jnp.float32)
    h = h.transpose(0, 2, 1, 3).reshape(bsz, s, MLSTM_WIDTH).astype(x.dtype)
    out_a = jax.nn.sigmoid(mo) * h
    def aheads(t):
        return t.reshape(bsz, s, ATTN_HEADS, ATTN_HEAD_DIM).transpose(0, 2, 1, 3)
    pos = jnp.arange(s, dtype=jnp.float32)
    qa = rope_partial(aheads(aq), pos)
    ka = rope_partial(aheads(ak), pos)
    oa = dilated_attention(qa, ka, aheads(av))
    out_b = oa.transpose(0, 2, 1, 3).reshape(bsz, s, ATTN_WIDTH)
    return jnp.concatenate([out_a, out_b], axis=-1) @ w_out


def peer_ffn(x, w_query, sub_keys_1, sub_keys_2, expert_u, expert_v):
    bsz, s, d = x.shape
    t = bsz * s
    xt = x.reshape(t, d)
    q = (xt @ w_query).reshape(t, PEER_HEADS, PEER_QUERY_DIM)
    half = PEER_QUERY_DIM // 2
    s1 = jnp.einsum('thd,kd->thk', q[..., :half], sub_keys_1).astype(jnp.float32)
    s2 = jnp.einsum('thd,kd->thk', q[..., half:], sub_keys_2).astype(jnp.float32)
    v1, i1 = lax.top_k(s1, PEER_TOPK)
    v2, i2 = lax.top_k(s2, PEER_TOPK)
    cand = (v1[..., :, None] + v2[..., None, :]).reshape(t, PEER_HEADS, PEER_TOPK * PEER_TOPK)
    sc, ci = lax.top_k(cand, PEER_TOPK)
    e1 = jnp.take_along_axis(i1, ci // PEER_TOPK, axis=-1)
    e2 = jnp.take_along_axis(i2, ci % PEER_TOPK, axis=-1)
    experts = e1 * PEER_KEYS + e2
    gates = jax.nn.softmax(sc, axis=-1).astype(x.dtype)
    nb = t // PEER_TOKEN_BLOCK
    def block(args):
        xb, eb, gb = args
        hu = jnp.einsum('thkd,td->thk', expert_u[eb], xb)
        act = jax.nn.gelu(hu, approximate=False) * gb
        return jnp.einsum('thk,thkd->td', act, expert_v[eb])
    out = lax.map(block, (xt.reshape(nb, PEER_TOKEN_BLOCK, d),
                          experts.reshape(nb, PEER_TOKEN_BLOCK, PEER_HEADS, PEER_TOPK),
                          gates.reshape(nb, PEER_TOKEN_BLOCK, PEER_HEADS, PEER_TOPK)))
    return out.reshape(bsz, s, d)


def setup_inputs(seed: int = 0) -> dict:
    key = jax.random.key(seed)
    ks = jax.random.split(key, 17)
    d = D_MODEL
    nrm = jax.random.normal
    x = nrm(ks[0], (BATCH, SEQ, d), jnp.float32)
    col_scale = jnp.concatenate([
        jnp.ones((2 * MLSTM_WIDTH,), jnp.float32),
        jnp.full((MLSTM_WIDTH,), DEEPNORM_BETA, jnp.float32),
        jnp.ones((MLSTM_WIDTH + 2 * MLSTM_HEADS + 2 * ATTN_WIDTH,), jnp.float32),
        jnp.full((ATTN_WIDTH,), DEEPNORM_BETA, jnp.float32)])
    w_in = nrm(ks[1], (DEPTH, d, IN_COLS), jnp.float32) * (d ** -0.5) * col_scale
    conv_w = nrm(ks[2], (DEPTH, CONV_WIDTH, 2 * MLSTM_WIDTH), jnp.float32) * (CONV_WIDTH ** -0.5)
    conv_b = 0.01 * nrm(ks[3], (DEPTH, 2 * MLSTM_WIDTH), jnp.float32)
    b_igate = 0.1 * nrm(ks[4], (DEPTH, MLSTM_HEADS), jnp.float32)
    b_fgate = 3.0 + 0.5 * nrm(ks[5], (DEPTH, MLSTM_HEADS), jnp.float32)
    mh_norm_g = 1.0 + 0.02 * nrm(ks[6], (DEPTH, MLSTM_WIDTH), jnp.float32)
    w_out = nrm(ks[7], (DEPTH, MIX_WIDTH, d), jnp.float32) * (MIX_WIDTH ** -0.5) * DEEPNORM_BETA
    ln1_g = 1.0 + 0.02 * nrm(ks[8], (DEPTH, d), jnp.float32)
    ln1_b = 0.01 * nrm(ks[9], (DEPTH, d), jnp.float32)
    w_query = nrm(ks[10], (DEPTH, d, PEER_HEADS * PEER_QUERY_DIM), jnp.float32) * (d ** -0.5)
    sub_keys_1 = nrm(ks[11], (DEPTH, PEER_KEYS, PEER_QUERY_DIM // 2), jnp.float32) * ((PEER_QUERY_DIM // 2) ** -0.5)
    sub_keys_2 = nrm(ks[12], (DEPTH, PEER_KEYS, PEER_QUERY_DIM // 2), jnp.float32) * ((PEER_QUERY_DIM // 2) ** -0.5)
    expert_u = nrm(ks[13], (DEPTH, PEER_EXPERTS, d), jnp.float32) * (d ** -0.5)
    expert_v = nrm(ks[14], (DEPTH, PEER_EXPERTS, d), jnp.float32) * ((PEER_HEADS * PEER_TOPK) ** -0.5) * DEEPNORM_BETA
    ln2_g = 1.0 + 0.02 * nrm(ks[15], (DEPTH, d), jnp.float32)
    ln2_b = 0.01 * nrm(ks[16], (DEPTH, d), jnp.float32)
    return {"x": x, "w_in": w_in, "conv_w": conv_w, "conv_b": conv_b, "b_igate": b_igate,
            "b_fgate": b_fgate, "mh_norm_g": mh_norm_g, "w_out": w_out, "ln1_g": ln1_g, "ln1_b": ln1_b,
            "w_query": w_query, "sub_keys_1": sub_keys_1, "sub_keys_2": sub_keys_2,
            "expert_u": expert_u, "expert_v": expert_v, "ln2_g": ln2_g, "ln2_b": ln2_b}


def reference(x, w_in, conv_w, conv_b, b_igate, b_fgate, mh_norm_g, w_out, ln1_g, ln1_b,
              w_query, sub_keys_1, sub_keys_2, expert_u, expert_v, ln2_g, ln2_b):
    for l in range(DEPTH):
        mix = token_mixer(x, w_in[l], conv_w[l], conv_b[l], b_igate[l], b_fgate[l], mh_norm_g[l], w_out[l])
        x = layer_norm(DEEPNORM_ALPHA * x + mix, ln1_g[l], ln1_b[l])
        ffn = peer_ffn(x, w_query[l], sub_keys_1[l], sub_keys_2[l], expert_u[l], expert_v[l])
        x = layer_norm(DEEPNORM_ALPHA * x + ffn, ln2_g[l], ln2_b[l])
    return x
```

```python
import functools

import jax
import jax.numpy as jnp
from jax import lax
from jax.experimental import pallas as pl
from jax.experimental.pallas import tpu as pltpu

F32 = jnp.float32
BF16 = jnp.bfloat16

MLSTM_HEADS = 4
MLSTM_HEAD_DIM = 256
MLSTM_WIDTH = MLSTM_HEADS * MLSTM_HEAD_DIM
CONV_WIDTH = 4
ATTN_HEADS = 8
ATTN_HEAD_DIM = 128
ATTN_WIDTH = ATTN_HEADS * ATTN_HEAD_DIM
DILATED_PATTERNS = ((128, 1), (512, 4), (2048, 16))
ATTN_BLOCK = 128
ROPE_THETA = 500000.0
ROT_DIM = ATTN_HEAD_DIM // 4
PEER_KEYS = 128
PEER_HEADS = 8
PEER_TOPK = 16
PEER_QUERY_DIM = 256
LN_EPS = 1e-5

LANES = 128
SUBLANES = 8
VMEM_LIMIT_BYTES = 56 * 1024 * 1024
NEG_BIG = -1e30

MLSTM_CHUNK = 256
NT_DIMS = (((1,), (1,)), ((), ()))
TN_DIMS = (((0,), (0,)), ((), ()))


def _params(semantics):
    return pltpu.CompilerParams(dimension_semantics=semantics, vmem_limit_bytes=VMEM_LIMIT_BYTES)


def _layer_norm_rows(y, gain, bias):
    mu = jnp.mean(y, axis=-1, keepdims=True)
    yc = y - mu
    var = jnp.mean(yc * yc, axis=-1, keepdims=True)
    return yc * lax.rsqrt(var + LN_EPS) * gain + bias


def _inproj_kernel(x_ref, w_ref, wg_ref, o_ref, g_ref):
    xb = x_ref[...].astype(BF16)
    o_ref[...] = jnp.dot(xb, w_ref[...], preferred_element_type=F32)

    @pl.when(pl.program_id(1) == 0)
    def _():
        g_ref[...] = jnp.dot(xb, wg_ref[...], preferred_element_type=F32)


def _in_projection(xt, w_main, w_gate):
    t, d = xt.shape
    n_main = w_main.shape[1]
    tm = min(t, 1024)
    tn = 1024
    return pl.pallas_call(
        _inproj_kernel,
        grid=(t // tm, n_main // tn),
        in_specs=[
            pl.BlockSpec((tm, d), lambda m, n: (m, 0)),
            pl.BlockSpec((d, tn), lambda m, n: (0, n)),
            pl.BlockSpec((d, LANES), lambda m, n: (0, 0)),
        ],
        out_specs=[
            pl.BlockSpec((tm, tn), lambda m, n: (m, n)),
            pl.BlockSpec((tm, LANES), lambda m, n: (m, 0)),
        ],
        out_shape=[
            jax.ShapeDtypeStruct((t, n_main), F32),
            jax.ShapeDtypeStruct((t, LANES), F32),
        ],
        compiler_params=_params(("parallel", "arbitrary")),
        name="in_proj",
    )(xt, w_main, w_gate)


def _conv_silu(x, tail_ref, w_ref, b_ref, col0):
    l, c = x.shape
    prev = tail_ref[...]
    row = lax.broadcasted_iota(jnp.int32, (SUBLANES, c), 0)
    y = x * w_ref[CONV_WIDTH - 1:CONV_WIDTH, col0:col0 + c] + b_ref[:, col0:col0 + c]
    for s in range(1, CONV_WIDTH):
        xr = pltpu.roll(x, s, 0)
        pr = pltpu.roll(prev, s, 0)
        top = jnp.where(row < s, pr, xr[0:SUBLANES])
        xs = jnp.concatenate([top, xr[SUBLANES:]], axis=0)
        y = y + xs * w_ref[CONV_WIDTH - 1 - s:CONV_WIDTH - s, col0:col0 + c]
    tail_ref[...] = x[l - SUBLANES:l]
    return y * jax.nn.sigmoid(y)


def _mlstm_kernel(q_ref, k_ref, v_ref, og_ref, g_ref, cw_ref, cb_ref, gb_ref, ng_ref, out_ref,
                  ct_scr, n_scr, m_scr, qtail_scr, ktail_scr):
    l = q_ref.shape[0]
    dh = MLSTM_HEAD_DIM

    @pl.when(pl.program_id(1) == 0)
    def _():
        ct_scr[...] = jnp.zeros_like(ct_scr)
        n_scr[...] = jnp.zeros_like(n_scr)
        m_scr[...] = jnp.zeros_like(m_scr)
        qtail_scr[...] = jnp.zeros_like(qtail_scr)
        ktail_scr[...] = jnp.zeros_like(ktail_scr)

    q_all = _conv_silu(q_ref[...], qtail_scr, cw_ref, cb_ref, 0)
    k_all = _conv_silu(k_ref[...], ktail_scr, cw_ref, cb_ref, MLSTM_WIDTH) * (dh ** -0.5)

    gb = g_ref[...] + gb_ref[...]
    log_f = jnp.minimum(gb, 0.0) - jnp.log1p(jnp.exp(-jnp.abs(gb)))
    row = lax.broadcasted_iota(jnp.int32, (l, l), 0)
    col = lax.broadcasted_iota(jnp.int32, (l, l), 1)
    causal = row >= col
    bcum = jnp.dot(causal.astype(F32), log_f, precision=lax.Precision.HIGHEST,
                   preferred_element_type=F32)
    bcum_t = bcum.T
    gb_t = gb.T

    for h in range(MLSTM_HEADS):
        sl = slice(h * dh, (h + 1) * dh)
        q = q_all[:, sl]
        k = k_all[:, sl]
        v = v_ref[:, sl]
        qb = q.astype(BF16)
        kb = k.astype(BF16)
        vb = v.astype(BF16)
        i_col = gb[:, h:h + 1]
        i_row = gb_t[h:h + 1, :]
        b_col = bcum[:, MLSTM_HEADS + h:MLSTM_HEADS + h + 1]
        b_row = bcum_t[MLSTM_HEADS + h:MLSTM_HEADS + h + 1, :]
        m_prev = m_scr[0:1, h:h + 1]
        ct = ct_scr[h]
        n_row = n_scr[0:1, sl]

        dmat = jnp.where(causal, b_col - b_row + i_row, NEG_BIG)
        m_inter = b_col + m_prev
        m_t = jnp.maximum(m_inter, jnp.max(dmat, axis=1, keepdims=True))
        wts = lax.dot_general(qb, kb, NT_DIMS, preferred_element_type=F32) * jnp.exp(dmat - m_t)
        inter = jnp.exp(m_inter - m_t)
        num = (jnp.dot(wts.astype(BF16), vb, preferred_element_type=F32)
               + inter * jnp.dot(qb, ct.astype(BF16), preferred_element_type=F32))
        den = (jnp.sum(wts, axis=1, keepdims=True)
               + inter * jnp.sum(q * n_row, axis=1, keepdims=True))
        hh = num / jnp.maximum(jnp.abs(den), jnp.exp(-m_t))

        mu = jnp.mean(hh, axis=1, keepdims=True)
        hc = hh - mu
        var = jnp.mean(hc * hc, axis=1, keepdims=True)
        hn = hc * lax.rsqrt(var + LN_EPS) * ng_ref[:, sl]
        out_ref[:, sl] = jax.nn.sigmoid(og_ref[:, sl]) * hn

        b_last = b_col[l - 1:l, :]
        g_row = b_last - b_row + i_row
        g_col = b_last - b_col + i_col
        m_new = jnp.maximum(b_last + m_prev, jnp.max(g_row, axis=1, keepdims=True))
        decay = jnp.exp(b_last + m_prev - m_new)
        kw = k * jnp.exp(g_col - m_new)
        ct_scr[h] = decay * ct + lax.dot_general(kw.astype(BF16), vb, TN_DIMS,
                                                 preferred_element_type=F32)
        n_scr[0:1, sl] = decay * n_row + jnp.sum(kw, axis=0, keepdims=True)
        m_scr[0:1, h:h + 1] = m_new


def _mlstm(proj, gates, conv_w, conv_b, gate_bias, norm_gain, bsz, seq):
    t = proj.shape[0]
    l = min(seq, MLSTM_CHUNK)
    nc = seq // l
    w = MLSTM_WIDTH

    def col_block(j):
        return pl.BlockSpec((l, w), lambda b, c, j=j: (b * nc + c, j))

    def whole(shape):
        return pl.BlockSpec(shape, lambda b, c: (0, 0))

    return pl.pallas_call(
        _mlstm_kernel,
        grid=(bsz, nc),
        in_specs=[
            col_block(0), col_block(1), col_block(2), col_block(3),
            pl.BlockSpec((l, LANES), lambda b, c: (b * nc + c, 0)),
            whole((CONV_WIDTH, 2 * w)), whole((1, 2 * w)), whole((1, LANES)), whole((1, w)),
        ],
        out_specs=pl.BlockSpec((l, w), lambda b, c: (b * nc + c, 0)),
        out_shape=jax.ShapeDtypeStruct((t, w), F32),
        scratch_shapes=[
            pltpu.VMEM((MLSTM_HEADS, MLSTM_HEAD_DIM, MLSTM_HEAD_DIM), F32),
            pltpu.VMEM((1, w), F32),
            pltpu.VMEM((1, LANES), F32),
            pltpu.VMEM((SUBLANES, w), F32),
            pltpu.VMEM((SUBLANES, w), F32),
        ],
        compiler_params=_params(("parallel", "arbitrary")),
        name="mlstm",
    )(proj, proj, proj, proj, gates, conv_w, conv_b, gate_bias, norm_gain)


ATTN_PAD = ATTN_BLOCK * max(d for _, d in DILATED_PATTERNS)
ATTN_PREP_ROWS = 512


def _attn_kernel(q_ref, k_ref, v_ref, cos_ref, sin_ref, o_ref, qs, ks, vs, m_scr, l_scr, acc_scr):
    s_len = q_ref.shape[0]
    half = ROT_DIM // 2
    blk = ATTN_BLOCK
    scale = ATTN_HEAD_DIM ** -0.5

    def rope(x, rows):
        lane = lax.broadcasted_iota(jnp.int32, x.shape, 1)
        swapped = jnp.where(lane < half, pltpu.roll(x, LANES - half, 1), pltpu.roll(x, half, 1))
        return x * cos_ref[rows, :] + swapped * sin_ref[rows, :]

    ks[0:ATTN_PAD, :] = jnp.zeros((ATTN_PAD, LANES), F32)
    vs[0:ATTN_PAD, :] = jnp.zeros((ATTN_PAD, LANES), F32)
    pr = min(ATTN_PREP_ROWS, s_len)
    for c in range(s_len // pr):
        rows = slice(c * pr, (c + 1) * pr)
        prows = slice(ATTN_PAD + c * pr, ATTN_PAD + (c + 1) * pr)
        qs[rows, :] = rope(q_ref[rows, :], rows)
        ks[prows, :] = rope(k_ref[rows, :], rows)
        vs[prows, :] = v_ref[rows, :]

    ii = lax.broadcasted_iota(jnp.int32, (blk, 2 * blk), 0)
    jj = lax.broadcasted_iota(jnp.int32, (blk, 2 * blk), 1)
    band = (jj >= ii) & (jj <= ii + blk)

    for pi, (_, dil) in enumerate(DILATED_PATTERNS):
        nb = s_len // (dil * blk)

        def block_body(n, r, dil=dil, first=(pi == 0)):
            q0 = r + dil * blk * n
            k0 = ATTN_PAD + r + dil * blk * (n - 1)
            if dil == 1:
                q_rows = pl.ds(q0, blk)
                k_rows = pl.ds(k0, 2 * blk)
            else:
                q_rows = pl.ds(q0, blk, stride=dil)
                k_rows = pl.ds(k0, 2 * blk, stride=dil)
            qb = qs[q_rows, :].astype(BF16)
            kb = ks[k_rows, :].astype(BF16)
            vb = vs[k_rows, :].astype(BF16)
            s = lax.dot_general(qb, kb, NT_DIMS, preferred_element_type=F32) * scale
            first_key = jnp.where(n > 0, 0, blk)
            s = jnp.where(band & (jj >= first_key), s, NEG_BIG)
            m_blk = jnp.max(s, axis=1, keepdims=True)
            p = jnp.exp(s - m_blk)
            l_blk = jnp.sum(p, axis=1, keepdims=True)
            o_blk = jnp.dot(p.astype(BF16), vb, preferred_element_type=F32)
            if first:
                m_scr[q_rows, :] = m_blk
                l_scr[q_rows, :] = l_blk
                acc_scr[q_rows, :] = o_blk
            else:
                m_old = m_scr[q_rows, :]
                m_new = jnp.maximum(m_old, m_blk)
                a_old = jnp.exp(m_old - m_new)
                a_blk = jnp.exp(m_blk - m_new)
                m_scr[q_rows, :] = m_new
                l_scr[q_rows, :] = a_old * l_scr[q_rows, :] + a_blk * l_blk
                acc_scr[q_rows, :] = a_old * acc_scr[q_rows, :] + a_blk * o_blk

        def residue_body(r, carry, nb=nb, block_body=block_body):
            def inner(n, c2):
                block_body(n, r)
                return c2
            return lax.fori_loop(0, nb, inner, carry)

        lax.fori_loop(0, dil, residue_body, 0)

    for c in range(s_len // pr):
        rows = slice(c * pr, (c + 1) * pr)
        o_ref[rows, :] = acc_scr[rows, :] / l_scr[rows, :]


def _dilated_attention(proj, cos_t, sin_t, bsz, seq):
    t = proj.shape[0]
    q_blk0 = 4 * MLSTM_WIDTH // ATTN_HEAD_DIM

    def head_block(group):
        return pl.BlockSpec((seq, ATTN_HEAD_DIM),
                            lambda b, h, group=group: (b, q_blk0 + group * ATTN_HEADS + h))

    table = pl.BlockSpec((seq, LANES), lambda b, h: (0, 0))
    return pl.pallas_call(
        _attn_kernel,
        grid=(bsz, ATTN_HEADS),
        in_specs=[head_block(0), head_block(1), head_block(2), table, table],
        out_specs=pl.BlockSpec((seq, ATTN_HEAD_DIM), lambda b, h: (b, h)),
        out_shape=jax.ShapeDtypeStruct((t, ATTN_WIDTH), F32),
        scratch_shapes=[
            pltpu.VMEM((seq, LANES), F32),
            pltpu.VMEM((seq + ATTN_PAD, LANES), F32),
            pltpu.VMEM((seq + ATTN_PAD, LANES), F32),
            pltpu.VMEM((seq, 1), F32),
            pltpu.VMEM((seq, 1), F32),
            pltpu.VMEM((seq, LANES), F32),
        ],
        compiler_params=_params(("parallel", "parallel")),
        name="dilated_attn",
    )(proj, proj, proj, cos_t, sin_t)


def _rope_tables(seq):
    half = ROT_DIM // 2
    inv = ROPE_THETA ** (-jnp.arange(half, dtype=F32) / half)
    ang = jnp.arange(seq, dtype=F32)[:, None] * inv[None, :]
    cos, sin = jnp.cos(ang), jnp.sin(ang)
    rest = LANES - ROT_DIM
    cos_t = jnp.concatenate([cos, cos, jnp.ones((seq, rest), F32)], axis=1)
    sin_t = jnp.concatenate([-sin, sin, jnp.zeros((seq, rest), F32)], axis=1)
    return cos_t, sin_t


def _outproj_ln_kernel(a_ref, b_ref, x_ref, w_ref, g_ref, beta_ref, o_ref, *, alpha):
    wa = a_ref.shape[1]
    mix = (jnp.dot(a_ref[...].astype(BF16), w_ref[0:wa, :], preferred_element_type=F32)
           + jnp.dot(b_ref[...].astype(BF16), w_ref[wa:, :], preferred_element_type=F32))
    o_ref[...] = _layer_norm_rows(alpha * x_ref[...] + mix, g_ref[...], beta_ref[...])


def _outproj_ln(out_a, out_b, xt, w_out, gain, bias, alpha):
    t, d = xt.shape
    tm = min(t, 256)
    wa, wb = out_a.shape[1], out_b.shape[1]
    return pl.pallas_call(
        functools.partial(_outproj_ln_kernel, alpha=alpha),
        grid=(t // tm,),
        in_specs=[
            pl.BlockSpec((tm, wa), lambda m: (m, 0)),
            pl.BlockSpec((tm, wb), lambda m: (m, 0)),
            pl.BlockSpec((tm, d), lambda m: (m, 0)),
            pl.BlockSpec((wa + wb, d), lambda m: (0, 0)),
            pl.BlockSpec((1, d), lambda m: (0, 0)),
            pl.BlockSpec((1, d), lambda m: (0, 0)),
        ],
        out_specs=pl.BlockSpec((tm, d), lambda m: (m, 0)),
        out_shape=jax.ShapeDtypeStruct((t, d), F32),
        compiler_params=_params(("parallel",)),
        name="out_proj_ln",
    )(out_a, out_b, xt, w_out, gain, bias)


def _extract_top(values, out_scr, count):
    cur = values
    for k in range(count):
        mk = jnp.max(cur, axis=0, keepdims=True)
        out_scr[k:k + 1, :] = mk
        cur = jnp.where(cur >= mk, NEG_BIG, cur)


def _peer_route_kernel(x_ref, wq_ref, k1_ref, k2_ref, u_ref, w_ref, tau_ref, rz_ref, v1_scr, v2_scr):
    half = PEER_QUERY_DIM // 2
    q = jnp.dot(x_ref[...].astype(BF16), wq_ref[...], preferred_element_type=F32)
    s1 = lax.dot_general(k1_ref[...], q[:, 0:half].astype(BF16), NT_DIMS, preferred_element_type=F32)
    s2 = lax.dot_general(k2_ref[...], q[:, half:].astype(BF16), NT_DIMS, preferred_element_type=F32)
    _extract_top(s1, v1_scr, PEER_TOPK)
    _extract_top(s2, v2_scr, PEER_TOPK)
    top1 = v1_scr[...]
    top2 = v2_scr[...]
    m1 = top1[0:1, :]
    m2 = top2[0:1, :]
    u_top = top1 - m1
    w_top = top2 - m2
    u_ref[0] = s1 - m1
    w_ref[0] = s2 - m2
    cands = [u_top + w_top[j:j + 1, :] for j in range(PEER_TOPK)]
    z = jnp.zeros_like(m1)
    mk = m1
    for _ in range(PEER_TOPK):
        best = functools.reduce(jnp.maximum, cands)
        mk = jnp.max(best, axis=0, keepdims=True)
        z = z + jnp.exp(mk)
        cands = [jnp.where(c >= mk, NEG_BIG, c) for c in cands]
    tau_ref[0] = mk
    rz_ref[0] = 1.0 / z


def _peer_route(x1, w_query, keys1, keys2):
    t, d = x1.shape
    tb = min(t, 512)
    qd = PEER_QUERY_DIM
    keys_spec = pl.BlockSpec((PEER_KEYS, qd // 2), lambda i, h: (0, 0))
    score_spec = pl.BlockSpec((1, PEER_KEYS, tb), lambda i, h: (h, 0, i))
    stat_spec = pl.BlockSpec((1, 1, tb), lambda i, h: (h, 0, i))
    return pl.pallas_call(
        _peer_route_kernel,
        grid=(t // tb, PEER_HEADS),
        in_specs=[
            pl.BlockSpec((tb, d), lambda i, h: (i, 0)),
            pl.BlockSpec((d, qd), lambda i, h: (0, h)),
            keys_spec, keys_spec,
        ],
        out_specs=[score_spec, score_spec, stat_spec, stat_spec],
        out_shape=[
            jax.ShapeDtypeStruct((PEER_HEADS, PEER_KEYS, t), F32),
            jax.ShapeDtypeStruct((PEER_HEADS, PEER_KEYS, t), F32),
            jax.ShapeDtypeStruct((PEER_HEADS, 1, t), F32),
            jax.ShapeDtypeStruct((PEER_HEADS, 1, t), F32),
        ],
        scratch_shapes=[pltpu.VMEM((PEER_TOPK, tb), F32), pltpu.VMEM((PEER_TOPK, tb), F32)],
        compiler_params=_params(("parallel", "arbitrary")),
        name="peer_route",
    )(x1, w_query, keys1, keys2)


PEER_EXPERT_BLOCK = 1024


def _peer_expert_kernel(x_ref, u_ref, w_ref, tau_ref, rz_ref, eu_ref, evt_ref, g_ref, beta_ref, o_ref,
                        xb_scr, acc_scr, act_scr, *, alpha):
    e = pl.program_id(1)

    @pl.when(e == 0)
    def _():
        xb_scr[...] = x_ref[...].astype(BF16)
        acc_scr[...] = jnp.zeros_like(acc_scr)

    hu = lax.dot_general(eu_ref[...], xb_scr[...], NT_DIMS, preferred_element_type=F32)
    for r in range(PEER_EXPERT_BLOCK // PEER_KEYS):
        rows = slice(r * PEER_KEYS, (r + 1) * PEER_KEYS)
        gate = jnp.zeros((PEER_KEYS, x_ref.shape[0]), F32)
        for h in range(PEER_HEADS):
            z = w_ref[h] + u_ref[h, r:r + 1, :]
            gate = gate + jnp.where(z >= tau_ref[h], jnp.exp(z), 0.0) * rz_ref[h]
        hr = hu[rows, :]
        act = 0.5 * hr * (1.0 + lax.erf(hr * (2.0 ** -0.5))) * gate
        act_scr[rows, :] = act.astype(BF16)
    acc_scr[...] += jnp.dot(evt_ref[...], act_scr[...], preferred_element_type=F32)

    @pl.when(e == pl.num_programs(1) - 1)
    def _():
        y = alpha * x_ref[...] + acc_scr[...].T
        o_ref[...] = _layer_norm_rows(y, g_ref[...], beta_ref[...])


def _peer_experts(x1, u_t, w_t, tau, rz, expert_u, expert_vt, gain, bias, alpha):
    t, d = x1.shape
    tb = min(t, 512)
    n_exp = expert_u.shape[0]
    eb = PEER_EXPERT_BLOCK
    rows_per_step = eb // PEER_KEYS
    stat_spec = pl.BlockSpec((PEER_HEADS, 1, tb), lambda i, e: (0, 0, i))
    return pl.pallas_call(
        functools.partial(_peer_expert_kernel, alpha=alpha),
        grid=(t // tb, n_exp // eb),
        in_specs=[
            pl.BlockSpec((tb, d), lambda i, e: (i, 0)),
            pl.BlockSpec((PEER_HEADS, rows_per_step, tb), lambda i, e: (0, e, i)),
            pl.BlockSpec((PEER_HEADS, PEER_KEYS, tb), lambda i, e: (0, 0, i)),
            stat_spec, stat_spec,
            pl.BlockSpec((eb, d), lambda i, e: (e, 0)),
            pl.BlockSpec((d, eb), lambda i, e: (0, e)),
            pl.BlockSpec((1, d), lambda i, e: (0, 0)),
            pl.BlockSpec((1, d), lambda i, e: (0, 0)),
        ],
        out_specs=pl.BlockSpec((tb, d), lambda i, e: (i, 0)),
        out_shape=jax.ShapeDtypeStruct((t, d), F32),
        scratch_shapes=[
            pltpu.VMEM((tb, d), BF16),
            pltpu.VMEM((d, tb), F32),
            pltpu.VMEM((eb, tb), BF16),
        ],
        compiler_params=_params(("parallel", "arbitrary")),
        name="peer_experts",
    )(x1, u_t, w_t, tau, rz, expert_u, expert_vt, gain, bias)


def _split_in_weights(w_in):
    mw, aw, nh = MLSTM_WIDTH, ATTN_WIDTH, MLSTM_HEADS
    gate0 = 4 * mw
    attn0 = gate0 + 2 * nh
    w_main = jnp.concatenate([w_in[:, :gate0], w_in[:, attn0:attn0 + 3 * aw]], axis=1)
    w_gate = jnp.pad(w_in[:, gate0:attn0], ((0, 0), (0, LANES - 2 * nh)))
    return w_main.astype(BF16), w_gate.astype(BF16)


def kernel(x, w_in, conv_w, conv_b, b_igate, b_fgate, mh_norm_g, w_out, ln1_g, ln1_b, w_query,
           sub_keys_1, sub_keys_2, expert_u, expert_v, ln2_g, ln2_b):
    bsz, seq, d = x.shape
    depth = w_in.shape[0]
    alpha = (2.0 * depth) ** 0.25
    cos_t, sin_t = _rope_tables(seq)
    xt = x.reshape(bsz * seq, d)
    for layer in range(depth):
        w_main, w_gate = _split_in_weights(w_in[layer])
        proj, gates = _in_projection(xt, w_main, w_gate)
        gate_bias = jnp.pad(jnp.concatenate([b_igate[layer], b_fgate[layer]]),
                            (0, LANES - 2 * MLSTM_HEADS)).reshape(1, LANES)
        out_a = _mlstm(proj, gates, conv_w[layer], conv_b[layer].reshape(1, -1), gate_bias,
                       mh_norm_g[layer].reshape(1, -1), bsz, seq)
        out_b = _dilated_attention(proj, cos_t, sin_t, bsz, seq)
        x1 = _outproj_ln(out_a, out_b, xt, w_out[layer].astype(BF16), ln1_g[layer].reshape(1, d),
                         ln1_b[layer].reshape(1, d), alpha)
        u_t, w_t, tau, rz = _peer_route(x1, w_query[layer].astype(BF16),
                                        sub_keys_1[layer].astype(BF16), sub_keys_2[layer].astype(BF16))
        xt = _peer_experts(x1, u_t, w_t, tau, rz, expert_u[layer].astype(BF16),
                           expert_v[layer].T.astype(BF16), ln2_g[layer].reshape(1, d),
                           ln2_b[layer].reshape(1, d), alpha)
    return xt.reshape(bsz, seq, d)
```

```python
import functools
import math

import jax
import jax.numpy as jnp
from jax import lax
from jax.experimental import pallas as pl
from jax.experimental.pallas import tpu as pltpu

F32 = jnp.float32
BF16 = jnp.bfloat16

MLSTM_HEADS = 4
MLSTM_HEAD_DIM = 256
MLSTM_WIDTH = MLSTM_HEADS * MLSTM_HEAD_DIM
CONV_WIDTH = 4
ATTN_HEADS = 8
ATTN_HEAD_DIM = 128
ATTN_WIDTH = ATTN_HEADS * ATTN_HEAD_DIM
DILATED_PATTERNS = ((128, 1), (512, 4), (2048, 16))
ATTN_BLOCK = 128
ROPE_THETA = 500000.0
ROT_DIM = ATTN_HEAD_DIM // 4
PEER_KEYS = 128
PEER_HEADS = 8
PEER_TOPK = 16
PEER_QUERY_DIM = 256
LN_EPS = 1e-5

LANES = 128
SUBLANES = 8
VMEM_LIMIT_BYTES = 56 * 1024 * 1024
NEG_BIG = -1e30
LOG2E = math.log2(math.e)

MLSTM_CHUNK = 256
NT_DIMS = (((1,), (1,)), ((), ()))
TN_DIMS = (((0,), (0,)), ((), ()))

ATTN_Q_GROUP = 4
ATTN_K_GROUP = 5
ATTN_V_GROUP = 6
MAIN_GROUPS = 7


def _params(semantics):
    return pltpu.CompilerParams(dimension_semantics=semantics, vmem_limit_bytes=VMEM_LIMIT_BYTES)


def _layer_norm_rows(y, gain, bias):
    mu = jnp.mean(y, axis=-1, keepdims=True)
    yc = y - mu
    var = jnp.mean(yc * yc, axis=-1, keepdims=True)
    return yc * lax.rsqrt(var + LN_EPS) * gain + bias


def _inproj_kernel(x_ref, w_ref, wg_ref, cos_ref, sin_ref, o_ref, g_ref):
    n = pl.program_id(1)
    xb = x_ref[...].astype(BF16)
    acc = jnp.dot(xb, w_ref[...], preferred_element_type=F32)
    rotary = (n == ATTN_Q_GROUP) | (n == ATTN_K_GROUP)

    @pl.when(jnp.logical_not(rotary))
    def _():
        o_ref[...] = acc

    @pl.when(rotary)
    def _():
        half = ROT_DIM // 2
        cos = cos_ref[...]
        sin = sin_ref[...]
        lane = lax.broadcasted_iota(jnp.int32, cos.shape, 1)
        for h in range(ATTN_HEADS):
            sl = slice(h * ATTN_HEAD_DIM, (h + 1) * ATTN_HEAD_DIM)
            xh = acc[:, sl]
            swapped = jnp.where(lane < half, pltpu.roll(xh, LANES - half, 1), pltpu.roll(xh, half, 1))
            o_ref[:, sl] = xh * cos + swapped * sin

    @pl.when(n == 0)
    def _():
        g_ref[...] = jnp.dot(xb, wg_ref[...], preferred_element_type=F32)


def _in_projection(xt, w_main, w_gate, cos_t, sin_t):
    t, d = xt.shape
    seq = cos_t.shape[0]
    n_main = w_main.shape[1]
    tm = min(seq, 1024)
    tn = 1024
    pos_blocks = seq // tm
    table = pl.BlockSpec((tm, LANES), lambda m, n: (m % pos_blocks, 0))
    return pl.pallas_call(
        _inproj_kernel,
        grid=(t // tm, n_main // tn),
        in_specs=[
            pl.BlockSpec((tm, d), lambda m, n: (m, 0)),
            pl.BlockSpec((d, tn), lambda m, n: (0, n)),
            pl.BlockSpec((d, LANES), lambda m, n: (0, 0)),
            table, table,
        ],
        out_specs=[
            pl.BlockSpec((tm, tn), lambda m, n: (m, n)),
            pl.BlockSpec((tm, LANES), lambda m, n: (m, 0)),
        ],
        out_shape=[
            jax.ShapeDtypeStruct((t, n_main), F32),
            jax.ShapeDtypeStruct((t, LANES), F32),
        ],
        compiler_params=_params(("parallel", "arbitrary")),
        name="in_proj",
    )(xt, w_main, w_gate, cos_t, sin_t)


def _rope_tables(seq):
    half = ROT_DIM // 2
    inv = ROPE_THETA ** (-jnp.arange(half, dtype=F32) / half)
    ang = jnp.arange(seq, dtype=F32)[:, None] * inv[None, :]
    cos, sin = jnp.cos(ang), jnp.sin(ang)
    rest = LANES - ROT_DIM
    cos_t = jnp.concatenate([cos, cos, jnp.ones((seq, rest), F32)], axis=1)
    sin_t = jnp.concatenate([-sin, sin, jnp.zeros((seq, rest), F32)], axis=1)
    return cos_t, sin_t


def _conv_silu(x, tail_ref, w_ref, b_ref, col0):
    l, c = x.shape
    prev = tail_ref[...]
    row = lax.broadcasted_iota(jnp.int32, (SUBLANES, c), 0)
    y = x * w_ref[CONV_WIDTH - 1:CONV_WIDTH, col0:col0 + c] + b_ref[:, col0:col0 + c]
    for s in range(1, CONV_WIDTH):
        xr = pltpu.roll(x, s, 0)
        pr = pltpu.roll(prev, s, 0)
        top = jnp.where(row < s, pr, xr[0:SUBLANES])
        xs = jnp.concatenate([top, xr[SUBLANES:]], axis=0)
        y = y + xs * w_ref[CONV_WIDTH - 1 - s:CONV_WIDTH - s, col0:col0 + c]
    tail_ref[...] = x[l - SUBLANES:l]
    return y * jax.nn.sigmoid(y)


def _mlstm_kernel(q_ref, k_ref, v_ref, og_ref, g_ref, cw_ref, cb_ref, gb_ref, ng_ref, out_ref,
                  ct_scr, n_scr, m_scr, qtail_scr, ktail_scr):
    l = q_ref.shape[0]
    dh = MLSTM_HEAD_DIM

    @pl.when(pl.program_id(1) == 0)
    def _():
        ct_scr[...] = jnp.zeros_like(ct_scr)
        n_scr[...] = jnp.zeros_like(n_scr)
        m_scr[...] = jnp.zeros_like(m_scr)
        qtail_scr[...] = jnp.zeros_like(qtail_scr)
        ktail_scr[...] = jnp.zeros_like(ktail_scr)

    q_all = _conv_silu(q_ref[...], qtail_scr, cw_ref, cb_ref, 0)
    k_all = _conv_silu(k_ref[...], ktail_scr, cw_ref, cb_ref, MLSTM_WIDTH) * (dh ** -0.5)

    gb = g_ref[...] + gb_ref[...]
    log_f = jnp.minimum(gb, 0.0) - jnp.log1p(jnp.exp(-jnp.abs(gb)))
    row = lax.broadcasted_iota(jnp.int32, (l, l), 0)
    col = lax.broadcasted_iota(jnp.int32, (l, l), 1)
    causal = row >= col
    bcum = jnp.dot(causal.astype(F32), log_f, precision=lax.Precision.HIGHEST,
                   preferred_element_type=F32)
    bcum_t = bcum.T
    gb_t = gb.T

    for h in range(MLSTM_HEADS):
        sl = slice(h * dh, (h + 1) * dh)
        q = q_all[:, sl]
        k = k_all[:, sl]
        v = v_ref[:, sl]
        qb = q.astype(BF16)
        kb = k.astype(BF16)
        vb = v.astype(BF16)
        i_col = gb[:, h:h + 1]
        i_row = gb_t[h:h + 1, :]
        b_col = bcum[:, MLSTM_HEADS + h:MLSTM_HEADS + h + 1]
        b_row = bcum_t[MLSTM_HEADS + h:MLSTM_HEADS + h + 1, :]
        m_prev = m_scr[0:1, h:h + 1]
        ct = ct_scr[h]
        n_row = n_scr[0:1, sl]

        dmat = jnp.where(causal, b_col - b_row + i_row, NEG_BIG)
        m_inter = b_col + m_prev
        m_t = jnp.maximum(m_inter, jnp.max(dmat, axis=1, keepdims=True))
        wts = lax.dot_general(qb, kb, NT_DIMS, preferred_element_type=F32) * jnp.exp(dmat - m_t)
        inter = jnp.exp(m_inter - m_t)
        num = (jnp.dot(wts.astype(BF16), vb, preferred_element_type=F32)
               + inter * jnp.dot(qb, ct.astype(BF16), preferred_element_type=F32))
        den = (jnp.sum(wts, axis=1, keepdims=True)
               + inter * jnp.sum(q * n_row, axis=1, keepdims=True))
        hh = num / jnp.maximum(jnp.abs(den), jnp.exp(-m_t))

        mu = jnp.mean(hh, axis=1, keepdims=True)
        hc = hh - mu
        var = jnp.mean(hc * hc, axis=1, keepdims=True)
        hn = hc * lax.rsqrt(var + LN_EPS) * ng_ref[:, sl]
        out_ref[:, sl] = jax.nn.sigmoid(og_ref[:, sl]) * hn

        b_last = b_col[l - 1:l, :]
        g_row = b_last - b_row + i_row
        g_col = b_last - b_col + i_col
        m_new = jnp.maximum(b_last + m_prev, jnp.max(g_row, axis=1, keepdims=True))
        decay = jnp.exp(b_last + m_prev - m_new)
        kw = k * jnp.exp(g_col - m_new)
        ct_scr[h] = decay * ct + lax.dot_general(kw.astype(BF16), vb, TN_DIMS,
                                                 preferred_element_type=F32)
        n_scr[0:1, sl] = decay * n_row + jnp.sum(kw, axis=0, keepdims=True)
        m_scr[0:1, h:h + 1] = m_new


def _mlstm(proj, gates, conv_w, conv_b, gate_bias, norm_gain, bsz, seq):
    t = proj.shape[0]
    l = min(seq, MLSTM_CHUNK)
    nc = seq // l
    w = MLSTM_WIDTH

    def col_block(j):
        return pl.BlockSpec((l, w), lambda b, c, j=j: (b * nc + c, j))

    def whole(shape):
        return pl.BlockSpec(shape, lambda b, c: (0, 0))

    return pl.pallas_call(
        _mlstm_kernel,
        grid=(bsz, nc),
        in_specs=[
            col_block(0), col_block(1), col_block(2), col_block(3),
            pl.BlockSpec((l, LANES), lambda b, c: (b * nc + c, 0)),
            whole((CONV_WIDTH, 2 * w)), whole((1, 2 * w)), whole((1, LANES)), whole((1, w)),
        ],
        out_specs=pl.BlockSpec((l, w), lambda b, c: (b * nc + c, 0)),
        out_shape=jax.ShapeDtypeStruct((t, w), F32),
        scratch_shapes=[
            pltpu.VMEM((MLSTM_HEADS, MLSTM_HEAD_DIM, MLSTM_HEAD_DIM), F32),
            pltpu.VMEM((1, w), F32),
            pltpu.VMEM((1, LANES), F32),
            pltpu.VMEM((SUBLANES, w), F32),
            pltpu.VMEM((SUBLANES, w), F32),
        ],
        compiler_params=_params(("parallel", "arbitrary")),
        name="mlstm",
    )(proj, proj, proj, proj, gates, conv_w, conv_b, gate_bias, norm_gain)


ATTN_CHAINS = 8
ATTN_STEP_ROWS_X_HEADS = 8192


def _attn_kernel(q_ref, k_ref, v_ref, o_ref, lse_ref, *, heads):
    blk = ATTN_BLOCK
    nb = q_ref.shape[0] // blk
    scale = ATTN_HEAD_DIM ** -0.5
    kk = lax.broadcasted_iota(jnp.int32, (2 * blk, blk), 0)
    qq = lax.broadcasted_iota(jnp.int32, (2 * blk, blk), 1)
    band = (kk >= qq) & (kk <= qq + blk)
    causal0 = (lax.broadcasted_iota(jnp.int32, (blk, blk), 0)
               <= lax.broadcasted_iota(jnp.int32, (blk, blk), 1))

    def block(hh, q0, k_rows, mask, n):
        cols = slice(hh * ATTN_HEAD_DIM, (hh + 1) * ATTN_HEAD_DIM)
        qb = q_ref[pl.ds(q0, blk), cols].astype(BF16)
        kb = k_ref[k_rows, cols].astype(BF16)
        vb = v_ref[k_rows, cols].astype(BF16)
        s_t = lax.dot_general(kb, qb, NT_DIMS, preferred_element_type=F32) * scale
        s_t = jnp.where(mask, s_t, NEG_BIG)
        m = jnp.max(s_t, axis=0, keepdims=True)
        p = jnp.exp(s_t - m)
        den = jnp.sum(p, axis=0, keepdims=True)
        pn = (p * (1.0 / den)).astype(BF16)
        o_ref[pl.ds(q0, blk), cols] = lax.dot_general(pn, vb, TN_DIMS, preferred_element_type=F32)
        lse_ref[hh, pl.ds(n, 1), :] = m + jnp.log(den)

    for hh in range(heads):
        block(hh, 0, pl.ds(0, blk), causal0, 0)

    def body(n, carry):
        q0 = pl.multiple_of(n * blk, blk)
        for hh in range(heads):
            block(hh, q0, pl.ds(q0 - blk, 2 * blk), band, n)
        return carry

    if nb > 1:
        lax.fori_loop(1, nb, body, 0, unroll=max(1, min(ATTN_CHAINS // heads, nb - 1)))


def _dilated_attention(proj, bsz, seq, dil):
    t, n_cols = proj.shape
    l = seq // dil
    nb = l // ATTN_BLOCK
    hp = max(1, min(ATTN_HEADS, ATTN_STEP_ROWS_X_HEADS // l))
    width = hp * ATTN_HEAD_DIM
    slabs = n_cols // width
    group_slabs = 1024 // width
    head_groups = ATTN_HEADS // hp
    view = proj.reshape(t // dil, dil * n_cols)

    def head_block(group):
        return pl.BlockSpec(
            (l, width), lambda b, g, r, group=group: (b, r * slabs + group * group_slabs + g))

    out, lse = pl.pallas_call(
        functools.partial(_attn_kernel, heads=hp),
        grid=(bsz, head_groups, dil),
        in_specs=[head_block(ATTN_Q_GROUP), head_block(ATTN_K_GROUP), head_block(ATTN_V_GROUP)],
        out_specs=[
            pl.BlockSpec((l, width), lambda b, g, r: (b, r * head_groups + g)),
            pl.BlockSpec((hp, nb, ATTN_BLOCK), lambda b, g, r: ((b * dil + r) * head_groups + g, 0, 0)),
        ],
        out_shape=[
            jax.ShapeDtypeStruct((t // dil, dil * ATTN_WIDTH), F32),
            jax.ShapeDtypeStruct((bsz * dil * ATTN_HEADS, nb, ATTN_BLOCK), F32),
        ],
        compiler_params=_params(("parallel", "parallel", "parallel")),
        name=f"dilated_attn_d{dil}",
    )(view, view, view)
    lse_tok = lse.reshape(bsz, dil, ATTN_HEADS, l).transpose(0, 3, 1, 2).reshape(t, ATTN_HEADS)
    return out.reshape(t, ATTN_WIDTH), lse_tok


def _outproj_ln_kernel(a_ref, o1_ref, o2_ref, o3_ref, lse_ref, x_ref, w_ref, g_ref, beta_ref, o_ref,
                       b_scr, *, alpha):
    wa = a_ref.shape[1]
    npat = len(DILATED_PATTERNS)
    lse = lse_ref[...]
    o_refs = (o1_ref, o2_ref, o3_ref)
    for h in range(ATTN_HEADS):
        sl = slice(h * ATTN_HEAD_DIM, (h + 1) * ATTN_HEAD_DIM)
        ls = [lse[:, p * ATTN_HEADS + h:p * ATTN_HEADS + h + 1] for p in range(npat)]
        mx = functools.reduce(jnp.maximum, ls)
        es = [jnp.exp(v - mx) for v in ls]
        inv = 1.0 / functools.reduce(lambda a, b: a + b, es)
        merged = sum((es[p] * inv) * o_refs[p][:, sl] for p in range(npat))
        b_scr[:, sl] = merged.astype(BF16)
    mix = (jnp.dot(a_ref[...].astype(BF16), w_ref[0:wa, :], preferred_element_type=F32)
           + jnp.dot(b_scr[...], w_ref[wa:, :], preferred_element_type=F32))
    o_ref[...] = _layer_norm_rows(alpha * x_ref[...] + mix, g_ref[...], beta_ref[...])


def _outproj_ln(out_a, attn_outs, lse_all, xt, w_out, gain, bias, alpha):
    t, d = xt.shape
    tm = min(t, 256)
    wa, wb = out_a.shape[1], attn_outs[0].shape[1]

    def rows(width):
        return pl.BlockSpec((tm, width), lambda m: (m, 0))

    def whole(shape):
        return pl.BlockSpec(shape, lambda m: (0, 0))

    return pl.pallas_call(
        functools.partial(_outproj_ln_kernel, alpha=alpha),
        grid=(t // tm,),
        in_specs=[rows(wa), rows(wb), rows(wb), rows(wb), rows(LANES), rows(d),
                  whole((wa + wb, d)), whole((1, d)), whole((1, d))],
        out_specs=rows(d),
        out_shape=jax.ShapeDtypeStruct((t, d), F32),
        scratch_shapes=[pltpu.VMEM((tm, wb), BF16)],
        compiler_params=_params(("parallel",)),
        name="out_proj_ln",
    )(out_a, *attn_outs, lse_all, xt, w_out, gain, bias)


def _compare_exchange(v, i, j):
    hi = jnp.maximum(v[i], v[j])
    lo = jnp.minimum(v[i], v[j])
    v[i], v[j] = hi, lo


def _bitonic_merge_desc(v):
    n = len(v)
    stride = n // 2
    while stride >= 1:
        for i in range(n):
            if i & stride == 0:
                _compare_exchange(v, i, i | stride)
        stride //= 2


def _bitonic_sort_desc(v):
    n = len(v)
    size = 2
    while size <= n:
        stride = size // 2
        while stride >= 1:
            for i in range(n):
                if i & stride == 0:
                    j = i | stride
                    _compare_exchange(v, i, j)
                    if i & size != 0:
                        v[i], v[j] = v[j], v[i]
            stride //= 2
        size *= 2


def _top16_and_next(groups):
    v = list(groups)
    _bitonic_sort_desc(v)
    n = len(v)
    dropped = None
    for shift in (4, 2, 1):
        other = [pltpu.roll(a, shift, 0) for a in v]
        hi = [jnp.maximum(v[i], other[n - 1 - i]) for i in range(n)]
        lo = [jnp.minimum(v[i], other[n - 1 - i]) for i in range(n)]
        lost = functools.reduce(jnp.maximum, lo)
        if dropped is not None:
            lost = jnp.maximum(lost, jnp.maximum(dropped, pltpu.roll(dropped, shift, 0)))
        dropped = lost
        v = hi
        _bitonic_merge_desc(v)
    return v, dropped


def _pack_sublanes(arrays, sub):
    out = arrays[0]
    for i in range(1, len(arrays)):
        out = jnp.where(sub == i, arrays[i], out)
    return out


def _peer_route_kernel(x_ref, wq_ref, k1_ref, k2_ref, u_ref, wz_ref, thr_ref):
    half = PEER_QUERY_DIM // 2
    k = PEER_TOPK
    q = jnp.dot(x_ref[...].astype(BF16), wq_ref[...], preferred_element_type=F32)
    s1 = lax.dot_general(k1_ref[...], q[:, 0:half].astype(BF16), NT_DIMS,
                         preferred_element_type=F32) * LOG2E
    s2 = lax.dot_general(k2_ref[...], q[:, half:].astype(BF16), NT_DIMS,
                         preferred_element_type=F32) * LOG2E
    groups = PEER_KEYS // SUBLANES
    top1, next1 = _top16_and_next([s1[g * SUBLANES:(g + 1) * SUBLANES] for g in range(groups)])
    top2, next2 = _top16_and_next([s2[g * SUBLANES:(g + 1) * SUBLANES] for g in range(groups)])

    sub = lax.broadcasted_iota(jnp.int32, top1[0].shape, 0)
    neg = jnp.full(top1[0].shape, NEG_BIG, F32)
    lo1 = _pack_sublanes(top1[0:SUBLANES], sub)
    hi1 = _pack_sublanes(top1[SUBLANES:k], sub)
    hi2 = _pack_sublanes(top2[SUBLANES:k], sub)
    cands = [lo1 + top2[0], hi1 + top2[0], lo1 + top2[1]]
    for b in range(2, SUBLANES):
        max_a = (k + 1) // (b + 1) - 1
        cands.append(jnp.where(sub <= max_a, lo1 + top2[b], neg))
    cands.append(top1[0] + hi2)
    cands.append(jnp.where(sub == 0, next1 + top2[0], jnp.where(sub == 1, top1[0] + next2, neg)))
    cands += [neg] * (k - len(cands))
    best, next_sum = _top16_and_next(cands)

    top_sum = best[0]
    z = functools.reduce(lambda a, b: a + b, [jnp.exp2(c - top_sum) for c in best])
    log_rz = -jnp.log2(z)
    thr = 0.5 * (best[k - 1] + next_sum) - top_sum + log_rz
    u_ref[0] = s1 - top1[0][0:1, :]
    wz_ref[0] = s2 - top2[0][0:1, :] + log_rz[0:1, :]
    thr_ref[0] = thr[0:1, :]


def _peer_route(x1, w_query, keys1, keys2):
    t, d = x1.shape
    tb = min(t, 512)
    qd = PEER_QUERY_DIM
    keys_spec = pl.BlockSpec((PEER_KEYS, qd // 2), lambda i, h: (0, 0))
    score_spec = pl.BlockSpec((1, PEER_KEYS, tb), lambda i, h: (h, 0, i))
    return pl.pallas_call(
        _peer_route_kernel,
        grid=(t // tb, PEER_HEADS),
        in_specs=[
            pl.BlockSpec((tb, d), lambda i, h: (i, 0)),
            pl.BlockSpec((d, qd), lambda i, h: (0, h)),
            keys_spec, keys_spec,
        ],
        out_specs=[score_spec, score_spec, pl.BlockSpec((1, 1, tb), lambda i, h: (h, 0, i))],
        out_shape=[
            jax.ShapeDtypeStruct((PEER_HEADS, PEER_KEYS, t), F32),
            jax.ShapeDtypeStruct((PEER_HEADS, PEER_KEYS, t), F32),
            jax.ShapeDtypeStruct((PEER_HEADS, 1, t), F32),
        ],
        compiler_params=_params(("parallel", "arbitrary")),
        name="peer_route",
    )(x1, w_query, keys1, keys2)


PEER_EXPERT_BLOCK = 1024
PEER_DOT_ROWS = 256


def _peer_expert_kernel(x_ref, u_ref, wz_ref, thr_ref, eu_ref, ev_ref, g_ref, beta_ref, o_ref,
                        xt_scr, acc_scr, act_scr, *, alpha):
    e = pl.program_id(1)
    tb = x_ref.shape[0]

    @pl.when(e == 0)
    def _():
        xt_scr[...] = x_ref[...].T.astype(BF16)
        acc_scr[...] = jnp.zeros_like(acc_scr)

    rows_per_dot = PEER_DOT_ROWS // PEER_KEYS
    for dblk in range(PEER_EXPERT_BLOCK // PEER_DOT_ROWS):
        hu = jnp.dot(eu_ref[dblk * PEER_DOT_ROWS:(dblk + 1) * PEER_DOT_ROWS, :], xt_scr[...],
                     preferred_element_type=F32)
        for rr in range(rows_per_dot):
            r = dblk * rows_per_dot + rr
            rows = slice(r * PEER_KEYS, (r + 1) * PEER_KEYS)
            for c in range(tb // LANES):
                cols = slice(c * LANES, (c + 1) * LANES)
                gate = None
                for h in range(PEER_HEADS):
                    z = wz_ref[h, :, cols] + u_ref[h, r:r + 1, cols]
                    g = jnp.where(z >= thr_ref[h, :, cols], jnp.exp2(z), 0.0)
                    gate = g if gate is None else gate + g
                hr = hu[rr * PEER_KEYS:(rr + 1) * PEER_KEYS, cols]
                act = 0.5 * hr * (1.0 + lax.erf(hr * (2.0 ** -0.5))) * gate
                act_scr[rows, cols] = act.astype(BF16)
    acc_scr[...] += lax.dot_general(act_scr[...], ev_ref[...], TN_DIMS, preferred_element_type=F32)

    @pl.when(e == pl.num_programs(1) - 1)
    def _():
        o_ref[...] = _layer_norm_rows(alpha * x_ref[...] + acc_scr[...], g_ref[...], beta_ref[...])


def _peer_experts(x1, u_t, wz_t, thr, expert_u, expert_v, gain, bias, alpha):
    t, d = x1.shape
    tb = min(t, 512)
    n_exp = expert_u.shape[0]
    eb = PEER_EXPERT_BLOCK
    rows_per_step = eb // PEER_KEYS
    return pl.pallas_call(
        functools.partial(_peer_expert_kernel, alpha=alpha),
        grid=(t // tb, n_exp // eb),
        in_specs=[
            pl.BlockSpec((tb, d), lambda i, e: (i, 0)),
            pl.BlockSpec((PEER_HEADS, rows_per_step, tb), lambda i, e: (0, e, i)),
            pl.BlockSpec((PEER_HEADS, PEER_KEYS, tb), lambda i, e: (0, 0, i)),
            pl.BlockSpec((PEER_HEADS, 1, tb), lambda i, e: (0, 0, i)),
            pl.BlockSpec((eb, d), lambda i, e: (e, 0)),
            pl.BlockSpec((eb, d), lambda i, e: (e, 0)),
            pl.BlockSpec((1, d), lambda i, e: (0, 0)),
            pl.BlockSpec((1, d), lambda i, e: (0, 0)),
        ],
        out_specs=pl.BlockSpec((tb, d), lambda i, e: (i, 0)),
        out_shape=jax.ShapeDtypeStruct((t, d), F32),
        scratch_shapes=[
            pltpu.VMEM((d, tb), BF16),
            pltpu.VMEM((tb, d), F32),
            pltpu.VMEM((eb, tb), BF16),
        ],
        compiler_params=_params(("parallel", "arbitrary")),
        name="peer_experts",
    )(x1, u_t, wz_t, thr, expert_u, expert_v, gain, bias)


def _split_in_weights(w_in):
    mw, aw, nh = MLSTM_WIDTH, ATTN_WIDTH, MLSTM_HEADS
    gate0 = 4 * mw
    attn0 = gate0 + 2 * nh
    w_main = jnp.concatenate([w_in[:, :gate0], w_in[:, attn0:attn0 + 3 * aw]], axis=1)
    w_gate = jnp.pad(w_in[:, gate0:attn0], ((0, 0), (0, LANES - 2 * nh)))
    return w_main.astype(BF16), w_gate.astype(BF16)


def kernel(x, w_in, conv_w, conv_b, b_igate, b_fgate, mh_norm_g, w_out, ln1_g, ln1_b, w_query,
           sub_keys_1, sub_keys_2, expert_u, expert_v, ln2_g, ln2_b):
    bsz, seq, d = x.shape
    depth = w_in.shape[0]
    alpha = (2.0 * depth) ** 0.25
    cos_t, sin_t = _rope_tables(seq)
    xt = x.reshape(bsz * seq, d)
    for layer in range(depth):
        w_main, w_gate = _split_in_weights(w_in[layer])
        proj, gates = _in_projection(xt, w_main, w_gate, cos_t, sin_t)
        gate_bias = jnp.pad(jnp.concatenate([b_igate[layer], b_fgate[layer]]),
                            (0, LANES - 2 * MLSTM_HEADS)).reshape(1, LANES)
        out_a = _mlstm(proj, gates, conv_w[layer], conv_b[layer].reshape(1, -1), gate_bias,
                       mh_norm_g[layer].reshape(1, -1), bsz, seq)
        attn = [_dilated_attention(proj, bsz, seq, dil) for _, dil in DILATED_PATTERNS]
        lse_all = jnp.concatenate([lse for _, lse in attn], axis=1)
        lse_all = jnp.pad(lse_all, ((0, 0), (0, LANES - lse_all.shape[1])))
        x1 = _outproj_ln(out_a, [o for o, _ in attn], lse_all, xt, w_out[layer].astype(BF16),
                         ln1_g[layer].reshape(1, d), ln1_b[layer].reshape(1, d), alpha)
        u_t, wz_t, thr = _peer_route(x1, w_query[layer].astype(BF16),
                                     sub_keys_1[layer].astype(BF16), sub_keys_2[layer].astype(BF16))
        xt = _peer_experts(x1, u_t, wz_t, thr, expert_u[layer].astype(BF16),
                           expert_v[layer].astype(BF16), ln2_g[layer].reshape(1, d),
                           ln2_b[layer].reshape(1, d), alpha)
    return xt.reshape(bsz, seq, d)
```

```python
import functools
import math

import jax
import jax.numpy as jnp
from jax import lax
from jax.experimental import pallas as pl
from jax.experimental.pallas import tpu as pltpu

F32 = jnp.float32
BF16 = jnp.bfloat16

MLSTM_HEADS = 4
MLSTM_HEAD_DIM = 256
MLSTM_WIDTH = MLSTM_HEADS * MLSTM_HEAD_DIM
CONV_WIDTH = 4
ATTN_HEADS = 8
ATTN_HEAD_DIM = 128
ATTN_WIDTH = ATTN_HEADS * ATTN_HEAD_DIM
DILATED_PATTERNS = ((128, 1), (512, 4), (2048, 16))
ATTN_BLOCK = 128
ROPE_THETA = 500000.0
ROT_DIM = ATTN_HEAD_DIM // 4
PEER_KEYS = 128
PEER_HEADS = 8
PEER_TOPK = 16
PEER_QUERY_DIM = 256
LN_EPS = 1e-5

LANES = 128
SUBLANES = 8
VMEM_LIMIT_BYTES = 56 * 1024 * 1024
NEG_BIG = -1e30
LOG2E = math.log2(math.e)

MLSTM_CHUNK = 256
NT_DIMS = (((1,), (1,)), ((), ()))
TN_DIMS = (((0,), (0,)), ((), ()))

ATTN_Q_GROUP = 4
ATTN_K_GROUP = 5
ATTN_V_GROUP = 6
MAIN_GROUPS = 7


def _params(semantics):
    return pltpu.CompilerParams(dimension_semantics=semantics, vmem_limit_bytes=VMEM_LIMIT_BYTES)


def _layer_norm_rows(y, gain, bias):
    mu = jnp.mean(y, axis=-1, keepdims=True)
    yc = y - mu
    var = jnp.mean(yc * yc, axis=-1, keepdims=True)
    return yc * lax.rsqrt(var + LN_EPS) * gain + bias


def _inproj_kernel(x_ref, w_ref, wg_ref, cos_ref, sin_ref, o_ref, g_ref):
    n = pl.program_id(1)
    xb = x_ref[...].astype(BF16)
    acc = jnp.dot(xb, w_ref[...], preferred_element_type=F32)
    rotary = (n == ATTN_Q_GROUP) | (n == ATTN_K_GROUP)

    @pl.when(jnp.logical_not(rotary))
    def _():
        o_ref[...] = acc

    @pl.when(rotary)
    def _():
        half = ROT_DIM // 2
        cos = cos_ref[...]
        sin = sin_ref[...]
        lane = lax.broadcasted_iota(jnp.int32, cos.shape, 1)
        for h in range(ATTN_HEADS):
            sl = slice(h * ATTN_HEAD_DIM, (h + 1) * ATTN_HEAD_DIM)
            xh = acc[:, sl]
            swapped = jnp.where(lane < half, pltpu.roll(xh, LANES - half, 1), pltpu.roll(xh, half, 1))
            o_ref[:, sl] = xh * cos + swapped * sin

    @pl.when(n == 0)
    def _():
        g_ref[...] = jnp.dot(xb, wg_ref[...], preferred_element_type=F32)


def _in_projection(xt, w_main, w_gate, cos_t, sin_t):
    t, d = xt.shape
    seq = cos_t.shape[0]
    n_main = w_main.shape[1]
    tm = min(seq, 1024)
    tn = 1024
    pos_blocks = seq // tm
    table = pl.BlockSpec((tm, LANES), lambda m, n: (m % pos_blocks, 0))
    return pl.pallas_call(
        _inproj_kernel,
        grid=(t // tm, n_main // tn),
        in_specs=[
            pl.BlockSpec((tm, d), lambda m, n: (m, 0)),
            pl.BlockSpec((d, tn), lambda m, n: (0, n)),
            pl.BlockSpec((d, LANES), lambda m, n: (0, 0)),
            table, table,
        ],
        out_specs=[
            pl.BlockSpec((tm, tn), lambda m, n: (m, n)),
            pl.BlockSpec((tm, LANES), lambda m, n: (m, 0)),
        ],
        out_shape=[
            jax.ShapeDtypeStruct((t, n_main), F32),
            jax.ShapeDtypeStruct((t, LANES), F32),
        ],
        compiler_params=_params(("parallel", "arbitrary")),
        name="in_proj",
    )(xt, w_main, w_gate, cos_t, sin_t)


def _rope_tables(seq):
    half = ROT_DIM // 2
    inv = ROPE_THETA ** (-jnp.arange(half, dtype=F32) / half)
    ang = jnp.arange(seq, dtype=F32)[:, None] * inv[None, :]
    cos, sin = jnp.cos(ang), jnp.sin(ang)
    rest = LANES - ROT_DIM
    cos_t = jnp.concatenate([cos, cos, jnp.ones((seq, rest), F32)], axis=1)
    sin_t = jnp.concatenate([-sin, sin, jnp.zeros((seq, rest), F32)], axis=1)
    return cos_t, sin_t


def _conv_silu(x, tail_ref, w_ref, b_ref, col0):
    l, c = x.shape
    prev = tail_ref[...]
    row = lax.broadcasted_iota(jnp.int32, (SUBLANES, c), 0)
    y = x * w_ref[CONV_WIDTH - 1:CONV_WIDTH, col0:col0 + c] + b_ref[:, col0:col0 + c]
    for s in range(1, CONV_WIDTH):
        xr = pltpu.roll(x, s, 0)
        pr = pltpu.roll(prev, s, 0)
        top = jnp.where(row < s, pr, xr[0:SUBLANES])
        xs = jnp.concatenate([top, xr[SUBLANES:]], axis=0)
        y = y + xs * w_ref[CONV_WIDTH - 1 - s:CONV_WIDTH - s, col0:col0 + c]
    tail_ref[...] = x[l - SUBLANES:l]
    return y * jax.nn.sigmoid(y)


def _mlstm_kernel(q_ref, k_ref, v_ref, og_ref, g_ref, cw_ref, cb_ref, gb_ref, ng_ref, out_ref,
                  ct_scr, n_scr, m_scr, qtail_scr, ktail_scr):
    l = q_ref.shape[0]
    dh = MLSTM_HEAD_DIM

    @pl.when(pl.program_id(1) == 0)
    def _():
        ct_scr[...] = jnp.zeros_like(ct_scr)
        n_scr[...] = jnp.zeros_like(n_scr)
        m_scr[...] = jnp.zeros_like(m_scr)
        qtail_scr[...] = jnp.zeros_like(qtail_scr)
        ktail_scr[...] = jnp.zeros_like(ktail_scr)

    q_all = _conv_silu(q_ref[...], qtail_scr, cw_ref, cb_ref, 0)
    k_all = _conv_silu(k_ref[...], ktail_scr, cw_ref, cb_ref, MLSTM_WIDTH) * (dh ** -0.5)

    gb = g_ref[...] + gb_ref[...]
    log_f = jnp.minimum(gb, 0.0) - jnp.log1p(jnp.exp(-jnp.abs(gb)))
    row = lax.broadcasted_iota(jnp.int32, (l, l), 0)
    col = lax.broadcasted_iota(jnp.int32, (l, l), 1)
    causal = row >= col
    bcum = jnp.dot(causal.astype(F32), log_f, precision=lax.Precision.HIGHEST,
                   preferred_element_type=F32)
    bcum_t = bcum.T
    gb_t = gb.T

    for h in range(MLSTM_HEADS):
        sl = slice(h * dh, (h + 1) * dh)
        q = q_all[:, sl]
        k = k_all[:, sl]
        v = v_ref[:, sl]
        qb = q.astype(BF16)
        kb = k.astype(BF16)
        vb = v.astype(BF16)
        i_col = gb[:, h:h + 1]
        i_row = gb_t[h:h + 1, :]
        b_col = bcum[:, MLSTM_HEADS + h:MLSTM_HEADS + h + 1]
        b_row = bcum_t[MLSTM_HEADS + h:MLSTM_HEADS + h + 1, :]
        m_prev = m_scr[0:1, h:h + 1]
        ct = ct_scr[h]
        n_row = n_scr[0:1, sl]

        dmat = jnp.where(causal, b_col - b_row + i_row, NEG_BIG)
        m_inter = b_col + m_prev
        m_t = jnp.maximum(m_inter, jnp.max(dmat, axis=1, keepdims=True))
        wts = lax.dot_general(qb, kb, NT_DIMS, preferred_element_type=F32) * jnp.exp(dmat - m_t)
        inter = jnp.exp(m_inter - m_t)
        num = (jnp.dot(wts.astype(BF16), vb, preferred_element_type=F32)
               + inter * jnp.dot(qb, ct.astype(BF16), preferred_element_type=F32))
        den = (jnp.sum(wts, axis=1, keepdims=True)
               + inter * jnp.sum(q * n_row, axis=1, keepdims=True))
        hh = num / jnp.maximum(jnp.abs(den), jnp.exp(-m_t))

        mu = jnp.mean(hh, axis=1, keepdims=True)
        hc = hh - mu
        var = jnp.mean(hc * hc, axis=1, keepdims=True)
        hn = hc * lax.rsqrt(var + LN_EPS) * ng_ref[:, sl]
        out_ref[:, sl] = jax.nn.sigmoid(og_ref[:, sl]) * hn

        b_last = b_col[l - 1:l, :]
        g_row = b_last - b_row + i_row
        g_col = b_last - b_col + i_col
        m_new = jnp.maximum(b_last + m_prev, jnp.max(g_row, axis=1, keepdims=True))
        decay = jnp.exp(b_last + m_prev - m_new)
        kw = k * jnp.exp(g_col - m_new)
        ct_scr[h] = decay * ct + lax.dot_general(kw.astype(BF16), vb, TN_DIMS,
                                                 preferred_element_type=F32)
        n_scr[0:1, sl] = decay * n_row + jnp.sum(kw, axis=0, keepdims=True)
        m_scr[0:1, h:h + 1] = m_new


def _mlstm(proj, gates, conv_w, conv_b, gate_bias, norm_gain, bsz, seq):
    t = proj.shape[0]
    l = min(seq, MLSTM_CHUNK)
    nc = seq // l
    w = MLSTM_WIDTH

    def col_block(j):
        return pl.BlockSpec((l, w), lambda b, c, j=j: (b * nc + c, j))

    def whole(shape):
        return pl.BlockSpec(shape, lambda b, c: (0, 0))

    return pl.pallas_call(
        _mlstm_kernel,
        grid=(bsz, nc),
        in_specs=[
            col_block(0), col_block(1), col_block(2), col_block(3),
            pl.BlockSpec((l, LANES), lambda b, c: (b * nc + c, 0)),
            whole((CONV_WIDTH, 2 * w)), whole((1, 2 * w)), whole((1, LANES)), whole((1, w)),
        ],
        out_specs=pl.BlockSpec((l, w), lambda b, c: (b * nc + c, 0)),
        out_shape=jax.ShapeDtypeStruct((t, w), F32),
        scratch_shapes=[
            pltpu.VMEM((MLSTM_HEADS, MLSTM_HEAD_DIM, MLSTM_HEAD_DIM), F32),
            pltpu.VMEM((1, w), F32),
            pltpu.VMEM((1, LANES), F32),
            pltpu.VMEM((SUBLANES, w), F32),
            pltpu.VMEM((SUBLANES, w), F32),
        ],
        compiler_params=_params(("parallel", "arbitrary")),
        name="mlstm",
    )(proj, proj, proj, proj, gates, conv_w, conv_b, gate_bias, norm_gain)


ATTN_CHAINS = 8
ATTN_STEP_ROWS_X_HEADS = 8192


def _attn_kernel(q_ref, k_ref, v_ref, o_ref, lse_ref, *, heads):
    blk = ATTN_BLOCK
    nb = q_ref.shape[0] // blk
    scale = ATTN_HEAD_DIM ** -0.5
    kk = lax.broadcasted_iota(jnp.int32, (2 * blk, blk), 0)
    qq = lax.broadcasted_iota(jnp.int32, (2 * blk, blk), 1)
    band = (kk >= qq) & (kk <= qq + blk)
    causal0 = (lax.broadcasted_iota(jnp.int32, (blk, blk), 0)
               <= lax.broadcasted_iota(jnp.int32, (blk, blk), 1))

    def block(hh, q0, k_rows, mask, n):
        cols = slice(hh * ATTN_HEAD_DIM, (hh + 1) * ATTN_HEAD_DIM)
        qb = q_ref[pl.ds(q0, blk), cols].astype(BF16)
        kb = k_ref[k_rows, cols].astype(BF16)
        vb = v_ref[k_rows, cols].astype(BF16)
        s_t = lax.dot_general(kb, qb, NT_DIMS, preferred_element_type=F32) * scale
        s_t = jnp.where(mask, s_t, NEG_BIG)
        m = jnp.max(s_t, axis=0, keepdims=True)
        p = jnp.exp(s_t - m)
        den = jnp.sum(p, axis=0, keepdims=True)
        pn = (p * (1.0 / den)).astype(BF16)
        o_ref[pl.ds(q0, blk), cols] = lax.dot_general(pn, vb, TN_DIMS, preferred_element_type=F32)
        lse_ref[hh, pl.ds(n, 1), :] = m + jnp.log(den)

    for hh in range(heads):
        block(hh, 0, pl.ds(0, blk), causal0, 0)

    def body(n, carry):
        q0 = pl.multiple_of(n * blk, blk)
        for hh in range(heads):
            block(hh, q0, pl.ds(q0 - blk, 2 * blk), band, n)
        return carry

    if nb > 1:
        lax.fori_loop(1, nb, body, 0, unroll=max(1, min(ATTN_CHAINS // heads, nb - 1)))


def _attn_dilated_kernel(q_ref, k_ref, v_ref, o_ref, lse_ref, *, heads, dil):
    blk = ATTN_BLOCK
    nb = q_ref.shape[0] // (dil * blk)
    nb_log2 = nb.bit_length() - 1
    assert nb == 1 << nb_log2 and nb >= 2
    scale = ATTN_HEAD_DIM ** -0.5
    kk = lax.broadcasted_iota(jnp.int32, (2 * blk, blk), 0)
    qq = lax.broadcasted_iota(jnp.int32, (2 * blk, blk), 1)

    def block(hh, idx):
        cols = slice(hh * ATTN_HEAD_DIM, (hh + 1) * ATTN_HEAD_DIM)
        r = lax.shift_right_logical(idx, nb_log2)
        n = jnp.bitwise_and(idx, nb - 1)
        lead = jnp.where(n > 0, blk, 0)
        q_rows = pl.ds(r + dil * blk * n, blk, stride=dil)
        k_rows = pl.ds(r + dil * (blk * n - lead), 2 * blk, stride=dil)
        mask = (kk <= qq + lead) & (kk >= qq + lead - blk)
        qb = q_ref[q_rows, cols].astype(BF16)
        kb = k_ref[k_rows, cols].astype(BF16)
        vb = v_ref[k_rows, cols].astype(BF16)
        s_t = lax.dot_general(kb, qb, NT_DIMS, preferred_element_type=F32) * scale
        s_t = jnp.where(mask, s_t, NEG_BIG)
        m = jnp.max(s_t, axis=0, keepdims=True)
        p = jnp.exp(s_t - m)
        den = jnp.sum(p, axis=0, keepdims=True)
        pn = (p * (1.0 / den)).astype(BF16)
        o_ref[q_rows, cols] = lax.dot_general(pn, vb, TN_DIMS, preferred_element_type=F32)
        lse_ref[hh, pl.ds(idx, 1), :] = m + jnp.log(den)

    def body(idx, carry):
        for hh in range(heads):
            block(hh, idx)
        return carry

    lax.fori_loop(0, dil * nb, body, 0, unroll=max(1, ATTN_CHAINS // heads))


def _dilated_attention(proj, bsz, seq, dil):
    t, n_cols = proj.shape
    l = seq // dil
    nb = l // ATTN_BLOCK
    hp = max(1, min(ATTN_HEADS, ATTN_STEP_ROWS_X_HEADS // seq)) if dil == 1 else 1
    width = hp * ATTN_HEAD_DIM
    group_slabs = 1024 // width
    head_groups = ATTN_HEADS // hp

    def head_block(group):
        return pl.BlockSpec((seq, width), lambda b, g, group=group: (b, group * group_slabs + g))

    if dil == 1:
        body = functools.partial(_attn_kernel, heads=hp)
    else:
        body = functools.partial(_attn_dilated_kernel, heads=hp, dil=dil)
    out, lse = pl.pallas_call(
        body,
        grid=(bsz, head_groups),
        in_specs=[head_block(ATTN_Q_GROUP), head_block(ATTN_K_GROUP), head_block(ATTN_V_GROUP)],
        out_specs=[
            pl.BlockSpec((seq, width), lambda b, g: (b, g)),
            pl.BlockSpec((hp, dil * nb, ATTN_BLOCK), lambda b, g: (b * head_groups + g, 0, 0)),
        ],
        out_shape=[
            jax.ShapeDtypeStruct((t, ATTN_WIDTH), F32),
            jax.ShapeDtypeStruct((bsz * ATTN_HEADS, dil * nb, ATTN_BLOCK), F32),
        ],
        compiler_params=_params(("parallel", "parallel")),
        name=f"dilated_attn_d{dil}",
    )(proj, proj, proj)
    lse_tok = lse.reshape(bsz, ATTN_HEADS, dil, l).transpose(0, 3, 2, 1).reshape(t, ATTN_HEADS)
    return out, lse_tok


def _outproj_ln_kernel(a_ref, o1_ref, o2_ref, o3_ref, lse_ref, x_ref, w_ref, g_ref, beta_ref, o_ref,
                       b_scr, *, alpha):
    wa = a_ref.shape[1]
    npat = len(DILATED_PATTERNS)
    lse = lse_ref[...]
    o_refs = (o1_ref, o2_ref, o3_ref)
    for h in range(ATTN_HEADS):
        sl = slice(h * ATTN_HEAD_DIM, (h + 1) * ATTN_HEAD_DIM)
        ls = [lse[:, p * ATTN_HEADS + h:p * ATTN_HEADS + h + 1] for p in range(npat)]
        mx = functools.reduce(jnp.maximum, ls)
        es = [jnp.exp(v - mx) for v in ls]
        inv = 1.0 / functools.reduce(lambda a, b: a + b, es)
        merged = sum((es[p] * inv) * o_refs[p][:, sl] for p in range(npat))
        b_scr[:, sl] = merged.astype(BF16)
    mix = (jnp.dot(a_ref[...].astype(BF16), w_ref[0:wa, :], preferred_element_type=F32)
           + jnp.dot(b_scr[...], w_ref[wa:, :], preferred_element_type=F32))
    o_ref[...] = _layer_norm_rows(alpha * x_ref[...] + mix, g_ref[...], beta_ref[...])


def _outproj_ln(out_a, attn_outs, lse_all, xt, w_out, gain, bias, alpha):
    t, d = xt.shape
    tm = min(t, 256)
    wa, wb = out_a.shape[1], attn_outs[0].shape[1]

    def rows(width):
        return pl.BlockSpec((tm, width), lambda m: (m, 0))

    def whole(shape):
        return pl.BlockSpec(shape, lambda m: (0, 0))

    return pl.pallas_call(
        functools.partial(_outproj_ln_kernel, alpha=alpha),
        grid=(t // tm,),
        in_specs=[rows(wa), rows(wb), rows(wb), rows(wb), rows(LANES), rows(d),
                  whole((wa + wb, d)), whole((1, d)), whole((1, d))],
        out_specs=rows(d),
        out_shape=jax.ShapeDtypeStruct((t, d), F32),
        scratch_shapes=[pltpu.VMEM((tm, wb), BF16)],
        compiler_params=_params(("parallel",)),
        name="out_proj_ln",
    )(out_a, *attn_outs, lse_all, xt, w_out, gain, bias)


def _compare_exchange(v, i, j):
    hi = jnp.maximum(v[i], v[j])
    lo = jnp.minimum(v[i], v[j])
    v[i], v[j] = hi, lo


def _bitonic_merge_desc(v):
    n = len(v)
    stride = n // 2
    while stride >= 1:
        for i in range(n):
            if i & stride == 0:
                _compare_exchange(v, i, i | stride)
        stride //= 2


def _bitonic_sort_desc(v):
    n = len(v)
    size = 2
    while size <= n:
        stride = size // 2
        while stride >= 1:
            for i in range(n):
                if i & stride == 0:
                    j = i | stride
                    _compare_exchange(v, i, j)
                    if i & size != 0:
                        v[i], v[j] = v[j], v[i]
            stride //= 2
        size *= 2


def _top16_and_next(groups):
    v = list(groups)
    _bitonic_sort_desc(v)
    n = len(v)
    dropped = None
    for shift in (4, 2, 1):
        other = [pltpu.roll(a, shift, 0) for a in v]
        hi = [jnp.maximum(v[i], other[n - 1 - i]) for i in range(n)]
        lo = [jnp.minimum(v[i], other[n - 1 - i]) for i in range(n)]
        lost = functools.reduce(jnp.maximum, lo)
        if dropped is not None:
            lost = jnp.maximum(lost, jnp.maximum(dropped, pltpu.roll(dropped, shift, 0)))
        dropped = lost
        v = hi
        _bitonic_merge_desc(v)
    return v, dropped


def _pack_sublanes(arrays, sub):
    out = arrays[0]
    for i in range(1, len(arrays)):
        out = jnp.where(sub == i, arrays[i], out)
    return out


def _peer_route_kernel(x_ref, wq_ref, k1_ref, k2_ref, u_ref, wz_ref, thr_ref):
    half = PEER_QUERY_DIM // 2
    k = PEER_TOPK
    q = jnp.dot(x_ref[...].astype(BF16), wq_ref[...], preferred_element_type=F32)
    s1 = lax.dot_general(k1_ref[...], q[:, 0:half].astype(BF16), NT_DIMS,
                         preferred_element_type=F32) * LOG2E
    s2 = lax.dot_general(k2_ref[...], q[:, half:].astype(BF16), NT_DIMS,
                         preferred_element_type=F32) * LOG2E
    groups = PEER_KEYS // SUBLANES
    top1, next1 = _top16_and_next([s1[g * SUBLANES:(g + 1) * SUBLANES] for g in range(groups)])
    top2, next2 = _top16_and_next([s2[g * SUBLANES:(g + 1) * SUBLANES] for g in range(groups)])

    sub = lax.broadcasted_iota(jnp.int32, top1[0].shape, 0)
    neg = jnp.full(top1[0].shape, NEG_BIG, F32)
    lo1 = _pack_sublanes(top1[0:SUBLANES], sub)
    hi1 = _pack_sublanes(top1[SUBLANES:k], sub)
    hi2 = _pack_sublanes(top2[SUBLANES:k], sub)
    cands = [lo1 + top2[0], hi1 + top2[0], lo1 + top2[1]]
    for b in range(2, SUBLANES):
        max_a = (k + 1) // (b + 1) - 1
        cands.append(jnp.where(sub <= max_a, lo1 + top2[b], neg))
    cands.append(top1[0] + hi2)
    cands.append(jnp.where(sub == 0, next1 + top2[0], jnp.where(sub == 1, top1[0] + next2, neg)))
    cands += [neg] * (k - len(cands))
    best, next_sum = _top16_and_next(cands)

    top_sum = best[0]
    z = functools.reduce(lambda a, b: a + b, [jnp.exp2(c - top_sum) for c in best])
    log_rz = -jnp.log2(z)
    thr = 0.5 * (best[k - 1] + next_sum) - top_sum + log_rz
    u_ref[0] = s1 - top1[0][0:1, :]
    wz_ref[0] = s2 - top2[0][0:1, :] + log_rz[0:1, :]
    thr_ref[0] = thr[0:1, :]


def _peer_route(x1, w_query, keys1, keys2):
    t, d = x1.shape
    tb = min(t, 512)
    qd = PEER_QUERY_DIM
    keys_spec = pl.BlockSpec((PEER_KEYS, qd // 2), lambda i, h: (0, 0))
    score_spec = pl.BlockSpec((1, PEER_KEYS, tb), lambda i, h: (h, 0, i))
    return pl.pallas_call(
        _peer_route_kernel,
        grid=(t // tb, PEER_HEADS),
        in_specs=[
            pl.BlockSpec((tb, d), lambda i, h: (i, 0)),
            pl.BlockSpec((d, qd), lambda i, h: (0, h)),
            keys_spec, keys_spec,
        ],
        out_specs=[score_spec, score_spec, pl.BlockSpec((1, 1, tb), lambda i, h: (h, 0, i))],
        out_shape=[
            jax.ShapeDtypeStruct((PEER_HEADS, PEER_KEYS, t), F32),
            jax.ShapeDtypeStruct((PEER_HEADS, PEER_KEYS, t), F32),
            jax.ShapeDtypeStruct((PEER_HEADS, 1, t), F32),
        ],
        compiler_params=_params(("parallel", "arbitrary")),
        name="peer_route",
    )(x1, w_query, keys1, keys2)


PEER_EXPERT_BLOCK = 1024
PEER_DOT_ROWS = 256


def _peer_expert_kernel(x_ref, u_ref, wz_ref, thr_ref, eu_ref, ev_ref, g_ref, beta_ref, o_ref,
                        xt_scr, acc_scr, act_scr, *, alpha):
    e = pl.program_id(1)
    tb = x_ref.shape[0]

    @pl.when(e == 0)
    def _():
        xt_scr[...] = x_ref[...].T.astype(BF16)
        acc_scr[...] = jnp.zeros_like(acc_scr)

    rows_per_dot = PEER_DOT_ROWS // PEER_KEYS
    for dblk in range(PEER_EXPERT_BLOCK // PEER_DOT_ROWS):
        hu = jnp.dot(eu_ref[dblk * PEER_DOT_ROWS:(dblk + 1) * PEER_DOT_ROWS, :], xt_scr[...],
                     preferred_element_type=F32)
        for rr in range(rows_per_dot):
            r = dblk * rows_per_dot + rr
            rows = slice(r * PEER_KEYS, (r + 1) * PEER_KEYS)
            for c in range(tb // LANES):
                cols = slice(c * LANES, (c + 1) * LANES)
                gate = None
                for h in range(PEER_HEADS):
                    z = wz_ref[h, :, cols] + u_ref[h, r:r + 1, cols]
                    g = jnp.where(z >= thr_ref[h, :, cols], jnp.exp2(z), 0.0)
                    gate = g if gate is None else gate + g
                hr = hu[rr * PEER_KEYS:(rr + 1) * PEER_KEYS, cols]
                act = 0.5 * hr * (1.0 + lax.erf(hr * (2.0 ** -0.5))) * gate
                act_scr[rows, cols] = act.astype(BF16)
    acc_scr[...] += lax.dot_general(act_scr[...], ev_ref[...], TN_DIMS, preferred_element_type=F32)

    @pl.when(e == pl.num_programs(1) - 1)
    def _():
        o_ref[...] = _layer_norm_rows(alpha * x_ref[...] + acc_scr[...], g_ref[...], beta_ref[...])


def _peer_experts(x1, u_t, wz_t, thr, expert_u, expert_v, gain, bias, alpha):
    t, d = x1.shape
    tb = min(t, 512)
    n_exp = expert_u.shape[0]
    eb = PEER_EXPERT_BLOCK
    rows_per_step = eb // PEER_KEYS
    return pl.pallas_call(
        functools.partial(_peer_expert_kernel, alpha=alpha),
        grid=(t // tb, n_exp // eb),
        in_specs=[
            pl.BlockSpec((tb, d), lambda i, e: (i, 0)),
            pl.BlockSpec((PEER_HEADS, rows_per_step, tb), lambda i, e: (0, e, i)),
            pl.BlockSpec((PEER_HEADS, PEER_KEYS, tb), lambda i, e: (0, 0, i)),
            pl.BlockSpec((PEER_HEADS, 1, tb), lambda i, e: (0, 0, i)),
            pl.BlockSpec((eb, d), lambda i, e: (e, 0)),
            pl.BlockSpec((eb, d), lambda i, e: (e, 0)),
            pl.BlockSpec((1, d), lambda i, e: (0, 0)),
            pl.BlockSpec((1, d), lambda i, e: (0, 0)),
        ],
        out_specs=pl.BlockSpec((tb, d), lambda i, e: (i, 0)),
        out_shape=jax.ShapeDtypeStruct((t, d), F32),
        scratch_shapes=[
            pltpu.VMEM((d, tb), BF16),
            pltpu.VMEM((tb, d), F32),
            pltpu.VMEM((eb, tb), BF16),
        ],
        compiler_params=_params(("parallel", "arbitrary")),
        name="peer_experts",
    )(x1, u_t, wz_t, thr, expert_u, expert_v, gain, bias)


def _split_in_weights(w_in):
    mw, aw, nh = MLSTM_WIDTH, ATTN_WIDTH, MLSTM_HEADS
    gate0 = 4 * mw
    attn0 = gate0 + 2 * nh
    w_main = jnp.concatenate([w_in[:, :gate0], w_in[:, attn0:attn0 + 3 * aw]], axis=1)
    w_gate = jnp.pad(w_in[:, gate0:attn0], ((0, 0), (0, LANES - 2 * nh)))
    return w_main.astype(BF16), w_gate.astype(BF16)


def kernel(x, w_in, conv_w, conv_b, b_igate, b_fgate, mh_norm_g, w_out, ln1_g, ln1_b, w_query,
           sub_keys_1, sub_keys_2, expert_u, expert_v, ln2_g, ln2_b):
    bsz, seq, d = x.shape
    depth = w_in.shape[0]
    alpha = (2.0 * depth) ** 0.25
    cos_t, sin_t = _rope_tables(seq)
    xt = x.reshape(bsz * seq, d)
    for layer in range(depth):
        w_main, w_gate = _split_in_weights(w_in[layer])
        proj, gates = _in_projection(xt, w_main, w_gate, cos_t, sin_t)
        gate_bias = jnp.pad(jnp.concatenate([b_igate[layer], b_fgate[layer]]),
                            (0, LANES - 2 * MLSTM_HEADS)).reshape(1, LANES)
        out_a = _mlstm(proj, gates, conv_w[layer], conv_b[layer].reshape(1, -1), gate_bias,
                       mh_norm_g[layer].reshape(1, -1), bsz, seq)
        attn = [_dilated_attention(proj, bsz, seq, dil) for _, dil in DILATED_PATTERNS]
        lse_all = jnp.concatenate([lse for _, lse in attn], axis=1)
        lse_all = jnp.pad(lse_all, ((0, 0), (0, LANES - lse_all.shape[1])))
        x1 = _outproj_ln(out_a, [o for o, _ in attn], lse_all, xt, w_out[layer].astype(BF16),
                         ln1_g[layer].reshape(1, d), ln1_b[layer].reshape(1, d), alpha)
        u_t, wz_t, thr = _peer_route(x1, w_query[layer].astype(BF16),
                                     sub_keys_1[layer].astype(BF16), sub_keys_2[layer].astype(BF16))
        xt = _peer_experts(x1, u_t, wz_t, thr, expert_u[layer].astype(BF16),
                           expert_v[layer].astype(BF16), ln2_g[layer].reshape(1, d),
                           ln2_b[layer].reshape(1, d), alpha)
    return xt.reshape(bsz, seq, d)
```

```python
import functools
import math

import jax
import jax.numpy as jnp
from jax import lax
from jax.experimental import pallas as pl
from jax.experimental.pallas import tpu as pltpu

F32 = jnp.float32
BF16 = jnp.bfloat16

MLSTM_HEADS = 4
MLSTM_HEAD_DIM = 256
MLSTM_WIDTH = MLSTM_HEADS * MLSTM_HEAD_DIM
CONV_WIDTH = 4
ATTN_HEADS = 8
ATTN_HEAD_DIM = 128
ATTN_WIDTH = ATTN_HEADS * ATTN_HEAD_DIM
DILATED_PATTERNS = ((128, 1), (512, 4), (2048, 16))
ATTN_BLOCK = 128
ROPE_THETA = 500000.0
ROT_DIM = ATTN_HEAD_DIM // 4
PEER_KEYS = 128
PEER_HEADS = 8
PEER_TOPK = 16
PEER_QUERY_DIM = 256
LN_EPS = 1e-5

LANES = 128
SUBLANES = 8
VMEM_LIMIT_BYTES = 56 * 1024 * 1024
NEG_BIG = -1e30
LOG2E = math.log2(math.e)

MLSTM_CHUNK = 256
NT_DIMS = (((1,), (1,)), ((), ()))
TN_DIMS = (((0,), (0,)), ((), ()))

ATTN_Q_GROUP = 4
ATTN_K_GROUP = 5
ATTN_V_GROUP = 6
MAIN_GROUPS = 7


def _params(semantics):
    return pltpu.CompilerParams(dimension_semantics=semantics, vmem_limit_bytes=VMEM_LIMIT_BYTES)


def _layer_norm_rows(y, gain, bias):
    mu = jnp.mean(y, axis=-1, keepdims=True)
    yc = y - mu
    var = jnp.mean(yc * yc, axis=-1, keepdims=True)
    return yc * lax.rsqrt(var + LN_EPS) * gain + bias


def _inproj_kernel(x_ref, w_ref, wg_ref, cos_ref, sin_ref, o_ref, g_ref):
    n = pl.program_id(1)
    xb = x_ref[...].astype(BF16)
    acc = jnp.dot(xb, w_ref[...], preferred_element_type=F32)
    rotary = (n == ATTN_Q_GROUP) | (n == ATTN_K_GROUP)

    @pl.when(jnp.logical_not(rotary))
    def _():
        o_ref[...] = acc

    @pl.when(rotary)
    def _():
        half = ROT_DIM // 2
        cos = cos_ref[...]
        sin = sin_ref[...]
        lane = lax.broadcasted_iota(jnp.int32, cos.shape, 1)
        for h in range(ATTN_HEADS):
            sl = slice(h * ATTN_HEAD_DIM, (h + 1) * ATTN_HEAD_DIM)
            xh = acc[:, sl]
            swapped = jnp.where(lane < half, pltpu.roll(xh, LANES - half, 1), pltpu.roll(xh, half, 1))
            o_ref[:, sl] = xh * cos + swapped * sin

    @pl.when(n == 0)
    def _():
        g_ref[...] = jnp.dot(xb, wg_ref[...], preferred_element_type=F32)


def _in_projection(xt, w_main, w_gate, cos_t, sin_t):
    t, d = xt.shape
    seq = cos_t.shape[0]
    n_main = w_main.shape[1]
    tm = min(seq, 1024)
    tn = 1024
    pos_blocks = seq // tm
    table = pl.BlockSpec((tm, LANES), lambda m, n: (m % pos_blocks, 0))
    return pl.pallas_call(
        _inproj_kernel,
        grid=(t // tm, n_main // tn),
        in_specs=[
            pl.BlockSpec((tm, d), lambda m, n: (m, 0)),
            pl.BlockSpec((d, tn), lambda m, n: (0, n)),
            pl.BlockSpec((d, LANES), lambda m, n: (0, 0)),
            table, table,
        ],
        out_specs=[
            pl.BlockSpec((tm, tn), lambda m, n: (m, n)),
            pl.BlockSpec((tm, LANES), lambda m, n: (m, 0)),
        ],
        out_shape=[
            jax.ShapeDtypeStruct((t, n_main), F32),
            jax.ShapeDtypeStruct((t, LANES), F32),
        ],
        compiler_params=_params(("parallel", "arbitrary")),
        name="in_proj",
    )(xt, w_main, w_gate, cos_t, sin_t)


def _rope_tables(seq):
    half = ROT_DIM // 2
    inv = ROPE_THETA ** (-jnp.arange(half, dtype=F32) / half)
    ang = jnp.arange(seq, dtype=F32)[:, None] * inv[None, :]
    cos, sin = jnp.cos(ang), jnp.sin(ang)
    rest = LANES - ROT_DIM
    cos_t = jnp.concatenate([cos, cos, jnp.ones((seq, rest), F32)], axis=1)
    sin_t = jnp.concatenate([-sin, sin, jnp.zeros((seq, rest), F32)], axis=1)
    return cos_t, sin_t


def _conv_silu(x, tail_ref, w_ref, b_ref, col0):
    l, c = x.shape
    prev = tail_ref[...]
    row = lax.broadcasted_iota(jnp.int32, (SUBLANES, c), 0)
    y = x * w_ref[CONV_WIDTH - 1:CONV_WIDTH, col0:col0 + c] + b_ref[:, col0:col0 + c]
    for s in range(1, CONV_WIDTH):
        xr = pltpu.roll(x, s, 0)
        pr = pltpu.roll(prev, s, 0)
        top = jnp.where(row < s, pr, xr[0:SUBLANES])
        xs = jnp.concatenate([top, xr[SUBLANES:]], axis=0)
        y = y + xs * w_ref[CONV_WIDTH - 1 - s:CONV_WIDTH - s, col0:col0 + c]
    tail_ref[...] = x[l - SUBLANES:l]
    return y * jax.nn.sigmoid(y)


def _mlstm_kernel(q_ref, k_ref, v_ref, og_ref, g_ref, cw_ref, cb_ref, gb_ref, ng_ref, out_ref,
                  ct_scr, n_scr, m_scr, qtail_scr, ktail_scr):
    l = q_ref.shape[0]
    dh = MLSTM_HEAD_DIM

    @pl.when(pl.program_id(1) == 0)
    def _():
        ct_scr[...] = jnp.zeros_like(ct_scr)
        n_scr[...] = jnp.zeros_like(n_scr)
        m_scr[...] = jnp.zeros_like(m_scr)
        qtail_scr[...] = jnp.zeros_like(qtail_scr)
        ktail_scr[...] = jnp.zeros_like(ktail_scr)

    q_all = _conv_silu(q_ref[...], qtail_scr, cw_ref, cb_ref, 0)
    k_all = _conv_silu(k_ref[...], ktail_scr, cw_ref, cb_ref, MLSTM_WIDTH) * (dh ** -0.5)

    gb = g_ref[...] + gb_ref[...]
    log_f = jnp.minimum(gb, 0.0) - jnp.log1p(jnp.exp(-jnp.abs(gb)))
    row = lax.broadcasted_iota(jnp.int32, (l, l), 0)
    col = lax.broadcasted_iota(jnp.int32, (l, l), 1)
    causal = row >= col
    bcum = jnp.dot(causal.astype(F32), log_f, precision=lax.Precision.HIGHEST,
                   preferred_element_type=F32)
    bcum_t = bcum.T
    gb_t = gb.T

    for h in range(MLSTM_HEADS):
        sl = slice(h * dh, (h + 1) * dh)
        q = q_all[:, sl]
        k = k_all[:, sl]
        v = v_ref[:, sl]
        qb = q.astype(BF16)
        kb = k.astype(BF16)
        vb = v.astype(BF16)
        i_col = gb[:, h:h + 1]
        i_row = gb_t[h:h + 1, :]
        b_col = bcum[:, MLSTM_HEADS + h:MLSTM_HEADS + h + 1]
        b_row = bcum_t[MLSTM_HEADS + h:MLSTM_HEADS + h + 1, :]
        m_prev = m_scr[0:1, h:h + 1]
        ct = ct_scr[h]
        n_row = n_scr[0:1, sl]

        dmat = jnp.where(causal, b_col - b_row + i_row, NEG_BIG)
        m_inter = b_col + m_prev
        m_t = jnp.maximum(m_inter, jnp.max(dmat, axis=1, keepdims=True))
        wts = lax.dot_general(qb, kb, NT_DIMS, preferred_element_type=F32) * jnp.exp(dmat - m_t)
        inter = jnp.exp(m_inter - m_t)
        num = (jnp.dot(wts.astype(BF16), vb, preferred_element_type=F32)
               + inter * jnp.dot(qb, ct.astype(BF16), preferred_element_type=F32))
        den = (jnp.sum(wts, axis=1, keepdims=True)
               + inter * jnp.sum(q * n_row, axis=1, keepdims=True))
        hh = num / jnp.maximum(jnp.abs(den), jnp.exp(-m_t))

        mu = jnp.mean(hh, axis=1, keepdims=True)
        hc = hh - mu
        var = jnp.mean(hc * hc, axis=1, keepdims=True)
        hn = hc * lax.rsqrt(var + LN_EPS) * ng_ref[:, sl]
        out_ref[:, sl] = jax.nn.sigmoid(og_ref[:, sl]) * hn

        b_last = b_col[l - 1:l, :]
        g_row = b_last - b_row + i_row
        g_col = b_last - b_col + i_col
        m_new = jnp.maximum(b_last + m_prev, jnp.max(g_row, axis=1, keepdims=True))
        decay = jnp.exp(b_last + m_prev - m_new)
        kw = k * jnp.exp(g_col - m_new)
        ct_scr[h] = decay * ct + lax.dot_general(kw.astype(BF16), vb, TN_DIMS,
                                                 preferred_element_type=F32)
        n_scr[0:1, sl] = decay * n_row + jnp.sum(kw, axis=0, keepdims=True)
        m_scr[0:1, h:h + 1] = m_new


def _mlstm(proj, gates, conv_w, conv_b, gate_bias, norm_gain, bsz, seq):
    t = proj.shape[0]
    l = min(seq, MLSTM_CHUNK)
    nc = seq // l
    w = MLSTM_WIDTH

    def col_block(j):
        return pl.BlockSpec((l, w), lambda b, c, j=j: (b * nc + c, j))

    def whole(shape):
        return pl.BlockSpec(shape, lambda b, c: (0, 0))

    return pl.pallas_call(
        _mlstm_kernel,
        grid=(bsz, nc),
        in_specs=[
            col_block(0), col_block(1), col_block(2), col_block(3),
            pl.BlockSpec((l, LANES), lambda b, c: (b * nc + c, 0)),
            whole((CONV_WIDTH, 2 * w)), whole((1, 2 * w)), whole((1, LANES)), whole((1, w)),
        ],
        out_specs=pl.BlockSpec((l, w), lambda b, c: (b * nc + c, 0)),
        out_shape=jax.ShapeDtypeStruct((t, w), F32),
        scratch_shapes=[
            pltpu.VMEM((MLSTM_HEADS, MLSTM_HEAD_DIM, MLSTM_HEAD_DIM), F32),
            pltpu.VMEM((1, w), F32),
            pltpu.VMEM((1, LANES), F32),
            pltpu.VMEM((SUBLANES, w), F32),
            pltpu.VMEM((SUBLANES, w), F32),
        ],
        compiler_params=_params(("parallel", "arbitrary")),
        name="mlstm",
    )(proj, proj, proj, proj, gates, conv_w, conv_b, gate_bias, norm_gain)


ATTN_CHAINS = 8
ATTN_STEP_ROWS_X_HEADS = 8192


def _attn_kernel(q_ref, k_ref, v_ref, o_ref, lse_ref, *, heads):
    blk = ATTN_BLOCK
    nb = q_ref.shape[0] // blk
    scale = ATTN_HEAD_DIM ** -0.5
    kk = lax.broadcasted_iota(jnp.int32, (2 * blk, blk), 0)
    qq = lax.broadcasted_iota(jnp.int32, (2 * blk, blk), 1)
    band = (kk >= qq) & (kk <= qq + blk)
    causal0 = (lax.broadcasted_iota(jnp.int32, (blk, blk), 0)
               <= lax.broadcasted_iota(jnp.int32, (blk, blk), 1))

    def block(hh, q0, k_rows, mask, n):
        cols = slice(hh * ATTN_HEAD_DIM, (hh + 1) * ATTN_HEAD_DIM)
        qb = q_ref[pl.ds(q0, blk), cols].astype(BF16)
        kb = k_ref[k_rows, cols].astype(BF16)
        vb = v_ref[k_rows, cols].astype(BF16)
        s_t = lax.dot_general(kb, qb, NT_DIMS, preferred_element_type=F32) * scale
        s_t = jnp.where(mask, s_t, NEG_BIG)
        m = jnp.max(s_t, axis=0, keepdims=True)
        p = jnp.exp(s_t - m)
        den = jnp.sum(p, axis=0, keepdims=True)
        pn = (p * (1.0 / den)).astype(BF16)
        o_ref[pl.ds(q0, blk), cols] = lax.dot_general(pn, vb, TN_DIMS, preferred_element_type=F32)
        lse_ref[hh, pl.ds(n, 1), :] = m + jnp.log(den)

    for hh in range(heads):
        block(hh, 0, pl.ds(0, blk), causal0, 0)

    def body(n, carry):
        q0 = pl.multiple_of(n * blk, blk)
        for hh in range(heads):
            block(hh, q0, pl.ds(q0 - blk, 2 * blk), band, n)
        return carry

    if nb > 1:
        lax.fori_loop(1, nb, body, 0, unroll=max(1, min(ATTN_CHAINS // heads, nb - 1)))


def _attn_dilated_kernel(q_ref, k_ref, v_ref, o_ref, lse_ref, *, heads, dil):
    blk = ATTN_BLOCK
    nb = q_ref.shape[0] // (dil * blk)
    nb_log2 = nb.bit_length() - 1
    assert nb == 1 << nb_log2 and nb >= 2
    scale = ATTN_HEAD_DIM ** -0.5
    kk = lax.broadcasted_iota(jnp.int32, (2 * blk, blk), 0)
    qq = lax.broadcasted_iota(jnp.int32, (2 * blk, blk), 1)

    def block(hh, idx):
        cols = slice(hh * ATTN_HEAD_DIM, (hh + 1) * ATTN_HEAD_DIM)
        r = lax.shift_right_logical(idx, nb_log2)
        n = jnp.bitwise_and(idx, nb - 1)
        lead = jnp.where(n > 0, blk, 0)
        q_rows = pl.ds(r + dil * blk * n, blk, stride=dil)
        k_rows = pl.ds(r + dil * (blk * n - lead), 2 * blk, stride=dil)
        mask = (kk <= qq + lead) & (kk >= qq + lead - blk)
        qb = q_ref[q_rows, cols].astype(BF16)
        kb = k_ref[k_rows, cols].astype(BF16)
        vb = v_ref[k_rows, cols].astype(BF16)
        s_t = lax.dot_general(kb, qb, NT_DIMS, preferred_element_type=F32) * scale
        s_t = jnp.where(mask, s_t, NEG_BIG)
        m = jnp.max(s_t, axis=0, keepdims=True)
        p = jnp.exp(s_t - m)
        den = jnp.sum(p, axis=0, keepdims=True)
        pn = (p * (1.0 / den)).astype(BF16)
        o_ref[q_rows, cols] = lax.dot_general(pn, vb, TN_DIMS, preferred_element_type=F32)
        lse_ref[hh, pl.ds(idx, 1), :] = m + jnp.log(den)

    def body(idx, carry):
        for hh in range(heads):
            block(hh, idx)
        return carry

    lax.fori_loop(0, dil * nb, body, 0, unroll=max(1, ATTN_CHAINS // heads))


def _dilated_attention(proj, bsz, seq, dil):
    t, n_cols = proj.shape
    l = seq // dil
    nb = l // ATTN_BLOCK
    hp = max(1, min(ATTN_HEADS, ATTN_STEP_ROWS_X_HEADS // seq)) if dil == 1 else 1
    width = hp * ATTN_HEAD_DIM
    group_slabs = 1024 // width
    head_groups = ATTN_HEADS // hp

    def head_block(group):
        return pl.BlockSpec((seq, width), lambda b, g, group=group: (b, group * group_slabs + g))

    if dil == 1:
        body = functools.partial(_attn_kernel, heads=hp)
    else:
        body = functools.partial(_attn_dilated_kernel, heads=hp, dil=dil)
    out, lse = pl.pallas_call(
        body,
        grid=(bsz, head_groups),
        in_specs=[head_block(ATTN_Q_GROUP), head_block(ATTN_K_GROUP), head_block(ATTN_V_GROUP)],
        out_specs=[
            pl.BlockSpec((seq, width), lambda b, g: (b, g)),
            pl.BlockSpec((hp, dil * nb, ATTN_BLOCK), lambda b, g: (b * head_groups + g, 0, 0)),
        ],
        out_shape=[
            jax.ShapeDtypeStruct((t, ATTN_WIDTH), F32),
            jax.ShapeDtypeStruct((bsz * ATTN_HEADS, dil * nb, ATTN_BLOCK), F32),
        ],
        compiler_params=_params(("parallel", "parallel")),
        name=f"dilated_attn_d{dil}",
    )(proj, proj, proj)
    lse_tok = lse.reshape(bsz, ATTN_HEADS, dil, l).transpose(0, 3, 2, 1).reshape(t, ATTN_HEADS)
    return out, lse_tok


OUTPROJ_ROW_GROUP = 128


def _outproj_ln_kernel(a_ref, o1_ref, o2_ref, o3_ref, lse_ref, x_ref, w_ref, g_ref, beta_ref, o_ref,
                       b_scr, *, alpha):
    wa = a_ref.shape[1]
    npat = len(DILATED_PATTERNS)
    o_refs = (o1_ref, o2_ref, o3_ref)
    for rg in range(a_ref.shape[0] // OUTPROJ_ROW_GROUP):
        rows = slice(rg * OUTPROJ_ROW_GROUP, (rg + 1) * OUTPROJ_ROW_GROUP)
        lse = lse_ref[rows, :]
        for h in range(ATTN_HEADS):
            sl = slice(h * ATTN_HEAD_DIM, (h + 1) * ATTN_HEAD_DIM)
            ls = [lse[:, p * ATTN_HEADS + h:p * ATTN_HEADS + h + 1] for p in range(npat)]
            mx = functools.reduce(jnp.maximum, ls)
            es = [jnp.exp(v - mx) for v in ls]
            inv = 1.0 / functools.reduce(lambda a, b: a + b, es)
            merged = sum((es[p] * inv) * o_refs[p][rows, sl] for p in range(npat))
            b_scr[rows, sl] = merged.astype(BF16)
        mix = (jnp.dot(a_ref[rows, :].astype(BF16), w_ref[0:wa, :], preferred_element_type=F32)
               + jnp.dot(b_scr[rows, :], w_ref[wa:, :], preferred_element_type=F32))
        o_ref[rows, :] = _layer_norm_rows(alpha * x_ref[rows, :] + mix, g_ref[...], beta_ref[...])


def _outproj_ln(out_a, attn_outs, lse_all, xt, w_out, gain, bias, alpha):
    t, d = xt.shape
    tm = min(t, 512)
    wa, wb = out_a.shape[1], attn_outs[0].shape[1]

    def rows(width):
        return pl.BlockSpec((tm, width), lambda m: (m, 0))

    def whole(shape):
        return pl.BlockSpec(shape, lambda m: (0, 0))

    return pl.pallas_call(
        functools.partial(_outproj_ln_kernel, alpha=alpha),
        grid=(t // tm,),
        in_specs=[rows(wa), rows(wb), rows(wb), rows(wb), rows(LANES), rows(d),
                  whole((wa + wb, d)), whole((1, d)), whole((1, d))],
        out_specs=rows(d),
        out_shape=jax.ShapeDtypeStruct((t, d), F32),
        scratch_shapes=[pltpu.VMEM((tm, wb), BF16)],
        compiler_params=_params(("parallel",)),
        name="out_proj_ln",
    )(out_a, *attn_outs, lse_all, xt, w_out, gain, bias)


def _compare_exchange(v, i, j):
    hi = jnp.maximum(v[i], v[j])
    lo = jnp.minimum(v[i], v[j])
    v[i], v[j] = hi, lo


def _bitonic_merge_desc(v):
    n = len(v)
    stride = n // 2
    while stride >= 1:
        for i in range(n):
            if i & stride == 0:
                _compare_exchange(v, i, i | stride)
        stride //= 2


def _bitonic_sort_desc(v):
    n = len(v)
    size = 2
    while size <= n:
        stride = size // 2
        while stride >= 1:
            for i in range(n):
                if i & stride == 0:
                    j = i | stride
                    _compare_exchange(v, i, j)
                    if i & size != 0:
                        v[i], v[j] = v[j], v[i]
            stride //= 2
        size *= 2


def _top16_and_next(groups):
    v = list(groups)
    _bitonic_sort_desc(v)
    n = len(v)
    dropped = None
    for shift in (4, 2, 1):
        other = [pltpu.roll(a, shift, 0) for a in v]
        hi = [jnp.maximum(v[i], other[n - 1 - i]) for i in range(n)]
        lo = [jnp.minimum(v[i], other[n - 1 - i]) for i in range(n)]
        lost = functools.reduce(jnp.maximum, lo)
        if dropped is not None:
            lost = jnp.maximum(lost, jnp.maximum(dropped, pltpu.roll(dropped, shift, 0)))
        dropped = lost
        v = hi
        _bitonic_merge_desc(v)
    return v, dropped


def _pack_sublanes(arrays, sub):
    out = arrays[0]
    for i in range(1, len(arrays)):
        out = jnp.where(sub == i, arrays[i], out)
    return out


def _peer_route_kernel(x_ref, wq_ref, k1_ref, k2_ref, roww_ref, rowc_ref, col_ref):
    q_all = jnp.dot(x_ref[...].astype(BF16), wq_ref[...], preferred_element_type=F32)
    for hh in range(PEER_ROUTE_HEADS):
        q = q_all[:, hh * PEER_QUERY_DIM:(hh + 1) * PEER_QUERY_DIM]
        _route_head(q, k1_ref, k2_ref, roww_ref, rowc_ref, col_ref, hh)


def _route_head(q, k1_ref, k2_ref, roww_ref, rowc_ref, col_ref, hh):
    half = PEER_QUERY_DIM // 2
    k = PEER_TOPK
    s1 = lax.dot_general(k1_ref[...], q[:, 0:half].astype(BF16), NT_DIMS,
                         preferred_element_type=F32) * LOG2E
    s2 = lax.dot_general(k2_ref[...], q[:, half:].astype(BF16), NT_DIMS,
                         preferred_element_type=F32) * LOG2E
    groups = PEER_KEYS // SUBLANES
    top1, next1 = _top16_and_next([s1[g * SUBLANES:(g + 1) * SUBLANES] for g in range(groups)])
    top2, next2 = _top16_and_next([s2[g * SUBLANES:(g + 1) * SUBLANES] for g in range(groups)])

    sub = lax.broadcasted_iota(jnp.int32, top1[0].shape, 0)
    neg = jnp.full(top1[0].shape, NEG_BIG, F32)
    lo1 = _pack_sublanes(top1[0:SUBLANES], sub)
    hi1 = _pack_sublanes(top1[SUBLANES:k], sub)
    hi2 = _pack_sublanes(top2[SUBLANES:k], sub)
    cands = [lo1 + top2[0], hi1 + top2[0], lo1 + top2[1]]
    for b in range(2, SUBLANES):
        max_a = (k + 1) // (b + 1) - 1
        cands.append(jnp.where(sub <= max_a, lo1 + top2[b], neg))
    cands.append(top1[0] + hi2)
    cands.append(jnp.where(sub == 0, next1 + top2[0], jnp.where(sub == 1, top1[0] + next2, neg)))
    cands += [neg] * (k - len(cands))
    best, next_sum = _top16_and_next(cands)

    top_sum = best[0]
    z = functools.reduce(lambda a, b: a + b, [jnp.exp2(c - top_sum) for c in best])
    log_rz = -jnp.log2(z)
    thr = 0.5 * (best[k - 1] + next_sum) - top_sum + log_rz
    u = s1 - top1[0][0:1, :]
    wz = s2 - top2[0][0:1, :] + log_rz[0:1, :]
    col_ref[hh] = jnp.exp2(wz)
    roww_ref[hh] = jnp.exp2(u - 1.0)
    rowc_ref[hh] = jnp.exp2(thr[0:1, :] - u)


PEER_ROUTE_HEADS = 2


def _peer_route(x1, w_query, keys1, keys2):
    t, d = x1.shape
    tb = min(t, 512)
    hp = PEER_ROUTE_HEADS
    qd = PEER_QUERY_DIM
    keys_spec = pl.BlockSpec((PEER_KEYS, qd // 2), lambda i, h: (0, 0))
    score_spec = pl.BlockSpec((hp, PEER_KEYS, tb), lambda i, h: (h, 0, i))
    return pl.pallas_call(
        _peer_route_kernel,
        grid=(t // tb, PEER_HEADS // hp),
        in_specs=[
            pl.BlockSpec((tb, d), lambda i, h: (i, 0)),
            pl.BlockSpec((d, hp * qd), lambda i, h: (0, h)),
            keys_spec, keys_spec,
        ],
        out_specs=[score_spec, score_spec, score_spec],
        out_shape=[jax.ShapeDtypeStruct((PEER_HEADS, PEER_KEYS, t), F32)] * 3,
        compiler_params=_params(("parallel", "arbitrary")),
        name="peer_route",
    )(x1, w_query, keys1, keys2)


PEER_EXPERT_BLOCK = 1024
PEER_DOT_ROWS = 256


def _peer_expert_kernel(x_ref, roww_ref, rowc_ref, col_ref, eu_ref, ev_ref, g_ref, beta_ref, o_ref,
                        xt_scr, acc_scr, act_scr, *, alpha):
    e = pl.program_id(1)
    tb = x_ref.shape[0]

    @pl.when(e == 0)
    def _():
        xt_scr[...] = x_ref[...].T.astype(BF16)
        acc_scr[...] = jnp.zeros_like(acc_scr)

    rows_per_dot = PEER_DOT_ROWS // PEER_KEYS
    for dblk in range(PEER_EXPERT_BLOCK // PEER_DOT_ROWS):
        hu = jnp.dot(eu_ref[dblk * PEER_DOT_ROWS:(dblk + 1) * PEER_DOT_ROWS, :], xt_scr[...],
                     preferred_element_type=F32)
        for rr in range(rows_per_dot):
            r = dblk * rows_per_dot + rr
            rows = slice(r * PEER_KEYS, (r + 1) * PEER_KEYS)
            for c in range(tb // LANES):
                cols = slice(c * LANES, (c + 1) * LANES)
                gate = None
                for h in range(PEER_HEADS):
                    col = col_ref[h, :, cols]
                    g = jnp.where(col >= rowc_ref[h, r:r + 1, cols], col, 0.0) * roww_ref[h, r:r + 1, cols]
                    gate = g if gate is None else gate + g
                hr = hu[rr * PEER_KEYS:(rr + 1) * PEER_KEYS, cols]
                act = hr * (1.0 + lax.erf(hr * (2.0 ** -0.5))) * gate
                act_scr[rows, cols] = act.astype(BF16)
    acc_scr[...] += lax.dot_general(act_scr[...], ev_ref[...], TN_DIMS, preferred_element_type=F32)

    @pl.when(e == pl.num_programs(1) - 1)
    def _():
        o_ref[...] = _layer_norm_rows(alpha * x_ref[...] + acc_scr[...], g_ref[...], beta_ref[...])


def _peer_experts(x1, row_w, row_c, col_v, expert_u, expert_v, gain, bias, alpha):
    t, d = x1.shape
    tb = min(t, 512)
    n_exp = expert_u.shape[0]
    eb = PEER_EXPERT_BLOCK
    rows_per_step = eb // PEER_KEYS
    return pl.pallas_call(
        functools.partial(_peer_expert_kernel, alpha=alpha),
        grid=(t // tb, n_exp // eb),
        in_specs=[
            pl.BlockSpec((tb, d), lambda i, e: (i, 0)),
            pl.BlockSpec((PEER_HEADS, rows_per_step, tb), lambda i, e: (0, e, i)),
            pl.BlockSpec((PEER_HEADS, rows_per_step, tb), lambda i, e: (0, e, i)),
            pl.BlockSpec((PEER_HEADS, PEER_KEYS, tb), lambda i, e: (0, 0, i)),
            pl.BlockSpec((eb, d), lambda i, e: (e, 0)),
            pl.BlockSpec((eb, d), lambda i, e: (e, 0)),
            pl.BlockSpec((1, d), lambda i, e: (0, 0)),
            pl.BlockSpec((1, d), lambda i, e: (0, 0)),
        ],
        out_specs=pl.BlockSpec((tb, d), lambda i, e: (i, 0)),
        out_shape=jax.ShapeDtypeStruct((t, d), F32),
        scratch_shapes=[
            pltpu.VMEM((d, tb), BF16),
            pltpu.VMEM((tb, d), F32),
            pltpu.VMEM((eb, tb), BF16),
        ],
        compiler_params=_params(("parallel", "arbitrary")),
        name="peer_experts",
    )(x1, row_w, row_c, col_v, expert_u, expert_v, gain, bias)


def _split_in_weights(w_in):
    mw, aw, nh = MLSTM_WIDTH, ATTN_WIDTH, MLSTM_HEADS
    gate0 = 4 * mw
    attn0 = gate0 + 2 * nh
    w_main = jnp.concatenate([w_in[:, :gate0], w_in[:, attn0:attn0 + 3 * aw]], axis=1)
    w_gate = jnp.pad(w_in[:, gate0:attn0], ((0, 0), (0, LANES - 2 * nh)))
    return w_main.astype(BF16), w_gate.astype(BF16)


def kernel(x, w_in, conv_w, conv_b, b_igate, b_fgate, mh_norm_g, w_out, ln1_g, ln1_b, w_query,
           sub_keys_1, sub_keys_2, expert_u, expert_v, ln2_g, ln2_b):
    bsz, seq, d = x.shape
    depth = w_in.shape[0]
    alpha = (2.0 * depth) ** 0.25
    cos_t, sin_t = _rope_tables(seq)
    xt = x.reshape(bsz * seq, d)
    for layer in range(depth):
        w_main, w_gate = _split_in_weights(w_in[layer])
        proj, gates = _in_projection(xt, w_main, w_gate, cos_t, sin_t)
        gate_bias = jnp.pad(jnp.concatenate([b_igate[layer], b_fgate[layer]]),
                            (0, LANES - 2 * MLSTM_HEADS)).reshape(1, LANES)
        out_a = _mlstm(proj, gates, conv_w[layer], conv_b[layer].reshape(1, -1), gate_bias,
                       mh_norm_g[layer].reshape(1, -1), bsz, seq)
        attn = [_dilated_attention(proj, bsz, seq, dil) for _, dil in DILATED_PATTERNS]
        lse_all = jnp.concatenate([lse for _, lse in attn], axis=1)
        lse_all = jnp.pad(lse_all, ((0, 0), (0, LANES - lse_all.shape[1])))
        x1 = _outproj_ln(out_a, [o for o, _ in attn], lse_all, xt, w_out[layer].astype(BF16),
                         ln1_g[layer].reshape(1, d), ln1_b[layer].reshape(1, d), alpha)
        row_w, row_c, col_v = _peer_route(x1, w_query[layer].astype(BF16),
                                          sub_keys_1[layer].astype(BF16),
                                          sub_keys_2[layer].astype(BF16))
        xt = _peer_experts(x1, row_w, row_c, col_v, expert_u[layer].astype(BF16),
                           expert_v[layer].astype(BF16), ln2_g[layer].reshape(1, d),
                           ln2_b[layer].reshape(1, d), alpha)
    return xt.reshape(bsz, seq, d)
```
